```python
import math
import jax
import jax.numpy as jnp
from jax import lax
import numpy as np

D_MODEL = 1024
BATCH = 8
SEQ = 8192
DEPTH = 1
DEC_BATCH = 2
DEC_SEQ = 8192
PAST_LEN = 128

HEAD_DIM = 64
A_GROUPS = ((128, 1), (512, 4), (2048, 16))
A_HEADS_PER_GROUP = 4
A_HEADS = A_HEADS_PER_GROUP * len(A_GROUPS)
A_WIDTH = A_HEADS * HEAD_DIM
A_OUT = A_HEADS_PER_GROUP * HEAD_DIM
A_QBLOCK = 64
B_HEADS = 8
B_WIDTH = B_HEADS * HEAD_DIM
GRID_W = 64
NA_ROWS = 8
NA_COLS = 16
NA_QROWS = 8
NA_QCOLS = 16
ROPE_THETA = 500000.0
ROPE_DIMS = HEAD_DIM // 4
D_FF = 4 * D_MODEL
IN_COLS = 3 * A_WIDTH + 3 * B_WIDTH + 2 * D_MODEL
LN_EPS = 1e-5
DEEPNORM_ALPHA = (2.0 * DEPTH) ** 0.25
DEEPNORM_BETA = (8.0 * DEPTH) ** -0.25
NEG_INF = -1e30

kernel_name = "hybrid_dilated_neighbourhood_encoder"


def layer_norm(x, g, b):
    xf = x.astype(jnp.float32)
    mu = jnp.mean(xf, axis=-1, keepdims=True)
    var = jnp.mean(jnp.square(xf - mu), axis=-1, keepdims=True)
    y = (xf - mu) * lax.rsqrt(var + LN_EPS) * g.astype(jnp.float32) + b.astype(jnp.float32)
    return y.astype(x.dtype)


def partial_rope(x):
    s = x.shape[1]
    half = ROPE_DIMS // 2
    inv = ROPE_THETA ** (-jnp.arange(half, dtype=jnp.float32) / half)
    ang = jnp.arange(s, dtype=jnp.float32)[:, None] * inv[None, :]
    cos = jnp.cos(ang)[None, :, None, :]
    sin = jnp.sin(ang)[None, :, None, :]
    xr = x[..., :ROPE_DIMS].astype(jnp.float32)
    x1, x2 = xr[..., :half], xr[..., half:]
    rot = jnp.concatenate([x1 * cos - x2 * sin, x1 * sin + x2 * cos], axis=-1).astype(x.dtype)
    return jnp.concatenate([rot, x[..., ROPE_DIMS:]], axis=-1)


def dilated_window_attention(q, k, v, window, dilation):
    b, s, h, dh = q.shape
    p = window // (2 * dilation)
    l = s // dilation
    qb = math.gcd(A_QBLOCK, l)
    nb = l // qb
    span = qb + 2 * p

    def to_sub(t):
        return t.reshape(b, l, dilation, h, dh).transpose(0, 2, 1, 3, 4)

    qs = to_sub(q).reshape(b, dilation, nb, qb, h, dh)
    pad = ((0, 0), (0, 0), (p, p), (0, 0), (0, 0))
    ks = jnp.pad(to_sub(k), pad)
    vs = jnp.pad(to_sub(v), pad)
    idx = jnp.arange(nb)[:, None] * qb + jnp.arange(span)[None, :]
    kb = jnp.take(ks, idx, axis=2)
    vb = jnp.take(vs, idx, axis=2)
    scores = jnp.einsum('brnqhd,brnkhd->brnhqk', qs, kb,
                        preferred_element_type=jnp.float32) * (dh ** -0.5)
    off = jnp.arange(span)[None, :] - p - jnp.arange(qb)[:, None]
    key_pos = idx - p
    in_seq = (key_pos >= 0) & (key_pos < l)
    valid = (jnp.abs(off) <= p)[None, :, :] & in_seq[:, None, :]
    scores = jnp.where(valid[None, None, :, None, :, :], scores, NEG_INF)
    m = jnp.max(scores, axis=-1, keepdims=True)
    e = jnp.exp(scores - m)
    den = jnp.sum(e, axis=-1, keepdims=True)
    o = jnp.einsum('brnhqk,brnkhd->brnqhd', e / den, vb.astype(jnp.float32))
    lse = (m + jnp.log(den))[..., 0]
    o = o.reshape(b, dilation, l, h, dh).transpose(0, 2, 1, 3, 4).reshape(b, s, h, dh)
    lse = lse.transpose(0, 1, 2, 4, 3).reshape(b, dilation, l, h).transpose(0, 2, 1, 3).reshape(b, s, h)
    return o, lse


def mixer_a(qa, ka, va):
    qa = partial_rope(qa)
    ka = partial_rope(ka)
    outs, lses = [], []
    for g, (window, dilation) in enumerate(A_GROUPS):
        sl = slice(g * A_HEADS_PER_GROUP, (g + 1) * A_HEADS_PER_GROUP)
        o, lse = dilated_window_attention(qa[:, :, sl], ka[:, :, sl], va[:, :, sl], window, dilation)
        outs.append(o)
        lses.append(lse)
    o = jnp.stack(outs, axis=0)
    w = jax.nn.softmax(jnp.stack(lses, axis=0), axis=0)
    return jnp.sum(w[..., None] * o, axis=0)


def neighbourhood_attention(q, k, v, rpb):
    b, s, h, dh = q.shape
    rows = s // GRID_W
    kh, kw = min(NA_ROWS, rows), NA_COLS
    qr, qc = math.gcd(NA_QROWS, rows), NA_QCOLS
    ksr, ksc = min(qr + kh - 1, rows), min(qc + kw - 1, GRID_W)
    nrb, ncb = rows // qr, GRID_W // qc
    r_idx = jnp.arange(rows)
    c_idx = jnp.arange(GRID_W)
    win_r = jnp.clip(r_idx - kh // 2, 0, rows - kh)
    win_c = jnp.clip(c_idx - kw // 2, 0, GRID_W - kw)
    key_r = jnp.minimum(win_r[::qr], rows - ksr)[:, None] + jnp.arange(ksr)[None, :]
    key_c = jnp.minimum(win_c[::qc], GRID_W - ksc)[:, None] + jnp.arange(ksc)[None, :]

    def gather_kv(t):
        t = t.reshape(b, rows, GRID_W, h, dh)
        t = jnp.take(jnp.take(t, key_r, axis=1), key_c, axis=3)
        return t.transpose(0, 1, 3, 2, 4, 5, 6).reshape(b, nrb, ncb, ksr * ksc, h, dh)

    kg = gather_kv(k)
    vg = gather_kv(v)
    qg = q.reshape(b, nrb, qr, ncb, qc, h, dh).transpose(0, 1, 3, 2, 4, 5, 6).reshape(b, nrb, ncb, qr * qc, h, dh)

    q_r = r_idx.reshape(nrb, qr)
    q_c = c_idx.reshape(ncb, qc)
    wr = win_r.reshape(nrb, qr)[:, :, None]
    wc = win_c.reshape(ncb, qc)[:, :, None]
    kr = key_r[:, None, :]
    kc = key_c[:, None, :]
    ok_r = (kr >= wr) & (kr < wr + kh)
    ok_c = (kc >= wc) & (kc < wc + kw)
    dr = jnp.clip(kr - q_r[:, :, None] + NA_ROWS - 1, 0, 2 * NA_ROWS - 2)
    dc = jnp.clip(kc - q_c[:, :, None] + NA_COLS - 1, 0, 2 * NA_COLS - 2)
    bias = rpb.astype(jnp.float32)[:, dr[:, :, :, None, None, None], dc[None, None, None, :, :, :]]
    ok = ok_r[:, :, :, None, None, None] & ok_c[None, None, None, :, :, :]
    bias = jnp.where(ok[None], bias, NEG_INF)
    bias = bias.transpose(1, 4, 0, 2, 5, 3, 6).reshape(nrb, ncb, h, qr * qc, ksr * ksc)

    scores = jnp.einsum('bnmqhd,bnmkhd->bnmhqk', qg, kg,
                        preferred_element_type=jnp.float32) * (dh ** -0.5) + bias[None]
    probs = jax.nn.softmax(scores, axis=-1)
    o = jnp.einsum('bnmhqk,bnmkhd->bnmqhd', probs, vg.astype(jnp.float32))
    o = o.reshape(b, nrb, ncb, qr, qc, h, dh).transpose(0, 1, 3, 2, 4, 5, 6)
    return o.reshape(b, s, h * dh)


def encoder_layer(x, w_in, b_gate, w_branch_a, w_branch_b, w_out, rpb,
                  ln1_g, ln1_b, w_ff1, b_ff1, w_ff2, b_ff2, ln2_g, ln2_b):
    bsz, seq, _ = x.shape
    proj = jnp.einsum('bsd,df->bsf', x, w_in)
    a, c = A_WIDTH, B_WIDTH
    cuts = [a, 2 * a, 3 * a, 3 * a + c, 3 * a + 2 * c, 3 * a + 3 * c, 3 * a + 3 * c + D_MODEL]
    qa, ka, va, qb, kb, vb, ga, gb = jnp.split(proj, cuts, axis=-1)
    ya = mixer_a(qa.reshape(bsz, seq, A_HEADS, HEAD_DIM),
                 ka.reshape(bsz, seq, A_HEADS, HEAD_DIM),
                 va.reshape(bsz, seq, A_HEADS, HEAD_DIM))
    ya = ya.reshape(bsz, seq, A_OUT).astype(x.dtype) @ w_branch_a
    yb = neighbourhood_attention(qb.reshape(bsz, seq, B_HEADS, HEAD_DIM),
                                 kb.reshape(bsz, seq, B_HEADS, HEAD_DIM),
                                 vb.reshape(bsz, seq, B_HEADS, HEAD_DIM), rpb)
    yb = yb.astype(x.dtype) @ w_branch_b
    merged = jax.nn.sigmoid(ga + b_gate[0]) * ya + jax.nn.sigmoid(gb + b_gate[1]) * yb
    x = layer_norm(DEEPNORM_ALPHA * x + merged @ w_out, ln1_g, ln1_b)
    hid = jnp.square(jax.nn.relu(x @ w_ff1 + b_ff1))
    x = layer_norm(DEEPNORM_ALPHA * x + hid @ w_ff2 + b_ff2, ln2_g, ln2_b)
    return x


def setup_inputs(seed: int = 0) -> dict:
    key = jax.random.key(seed)
    ks = jax.random.split(key, 16)

    def nrm(k, shape, scale):
        return jax.random.normal(k, shape, jnp.float32) * scale

    return {
        'x_prompt': nrm(ks[0], (BATCH, SEQ, D_MODEL), 1.0),
        'x_sample': nrm(ks[1], (DEC_BATCH, DEC_SEQ, D_MODEL), 1.0),
        'w_in': nrm(ks[2], (DEPTH, D_MODEL, IN_COLS), D_MODEL ** -0.5),
        'b_gate': nrm(ks[3], (DEPTH, 2, D_MODEL), 0.1),
        'w_branch_a': nrm(ks[4], (DEPTH, A_OUT, D_MODEL), A_OUT ** -0.5),
        'w_branch_b': nrm(ks[5], (DEPTH, B_WIDTH, D_MODEL), B_WIDTH ** -0.5),
        'w_out': nrm(ks[6], (DEPTH, D_MODEL, D_MODEL), DEEPNORM_BETA * D_MODEL ** -0.5),
        'rel_pos_bias': nrm(ks[7], (DEPTH, B_HEADS, 2 * NA_ROWS - 1, 2 * NA_COLS - 1), 0.1),
        'ln1_g': 1.0 + nrm(ks[8], (DEPTH, D_MODEL), 0.02),
        'ln1_b': nrm(ks[9], (DEPTH, D_MODEL), 0.02),
        'w_ff1': nrm(ks[10], (DEPTH, D_MODEL, D_FF), D_MODEL ** -0.5),
        'b_ff1': nrm(ks[11], (DEPTH, D_FF), 0.02),
        'w_ff2': nrm(ks[12], (DEPTH, D_FF, D_MODEL), DEEPNORM_BETA * D_FF ** -0.5),
        'b_ff2': nrm(ks[13], (DEPTH, D_MODEL), 0.02),
        'ln2_g': 1.0 + nrm(ks[14], (DEPTH, D_MODEL), 0.02),
        'ln2_b': nrm(ks[15], (DEPTH, D_MODEL), 0.02),
    }


def reference(x_prompt, x_sample, w_in, b_gate, w_branch_a, w_branch_b, w_out, rel_pos_bias,
              ln1_g, ln1_b, w_ff1, b_ff1, w_ff2, b_ff2, ln2_g, ln2_b):
    y_prompt = x_prompt
    y_sample = x_sample
    for layer in range(DEPTH):
        p = (w_in[layer], b_gate[layer], w_branch_a[layer], w_branch_b[layer], w_out[layer],
             rel_pos_bias[layer], ln1_g[layer], ln1_b[layer], w_ff1[layer], b_ff1[layer],
             w_ff2[layer], b_ff2[layer], ln2_g[layer], ln2_b[layer])
        y_prompt = encoder_layer(y_prompt, *p)
        y_sample = encoder_layer(y_sample, *p)
    return (y_prompt, y_sample)
```

```python
import functools
import math

import numpy as np
import jax
import jax.numpy as jnp
from jax import lax
from jax.experimental import pallas as pl
from jax.experimental.pallas import tpu as pltpu

D_MODEL = 1024
HEAD_DIM = 64
A_GROUPS = ((128, 1), (512, 4), (2048, 16))
A_HEADS_PER_GROUP = 4
A_GROUP_WIDTH = A_HEADS_PER_GROUP * HEAD_DIM
A_WIDTH = len(A_GROUPS) * A_GROUP_WIDTH
A_SIDE = 64
A_QBLOCK = 128
A_KWIN = A_QBLOCK + 2 * A_SIDE
B_HEADS = 8
B_WIDTH = B_HEADS * HEAD_DIM
GRID_W = 64
NA_ROWS = 8
NA_COLS = 16
NA_QROWS = 8
NA_QCOLS = 16
NA_KROWS = 16
NA_KCOLS = 32
NA_PIECE_ROWS = 4
ROPE_THETA = 500000.0
ROPE_DIMS = HEAD_DIM // 4
D_FF = 4 * D_MODEL
QKV_COLS = 3 * A_WIDTH + 3 * B_WIDTH
LN_EPS = 1e-5
NEG_INF = -1e30
LANES = 128

VMEM_LIMIT = 56 * 1024 * 1024

_BF16 = jnp.bfloat16
_F32 = jnp.float32


def _dot(a, b):
    return jnp.dot(a, b, preferred_element_type=_F32)


def _dot_nt(a, b):
    return lax.dot_general(a, b, (((1,), (1,)), ((), ())), preferred_element_type=_F32)


def _resident(shape):
    nd = len(shape)
    return pl.BlockSpec(shape, lambda *_: (0,) * nd, pipeline_mode=pl.Buffered(1))


def _proj_kernel(x_ref, w_ref, cos_ref, sa_ref, sb_ref,
                 qa_ref, ka_ref, va_ref, qb_ref, kb_ref, vb_ref):
    xb = x_ref[...].astype(_BF16)
    cos = cos_ref[...]
    sa = sa_ref[...]
    sb = sb_ref[...]

    def rope(acc):
        outs = []
        for c in range(A_WIDTH // LANES):
            ch = acc[:, c * LANES:(c + 1) * LANES]
            up = pltpu.roll(ch, LANES - ROPE_DIMS // 2, 1)
            dn = pltpu.roll(ch, ROPE_DIMS // 2, 1)
            outs.append(ch * cos + up * sa + dn * sb)
        return jnp.concatenate(outs, axis=1)

    a = A_WIDTH
    c = B_WIDTH
    qa_ref[...] = rope(_dot(xb, w_ref[:, 0:a])).astype(_BF16)
    ka_ref[...] = rope(_dot(xb, w_ref[:, a:2 * a])).astype(_BF16)
    va_ref[...] = _dot(xb, w_ref[:, 2 * a:3 * a]).astype(_BF16)
    qb_ref[...] = _dot(xb, w_ref[:, 3 * a:3 * a + c]).astype(_BF16)
    kb_ref[...] = _dot(xb, w_ref[:, 3 * a + c:3 * a + 2 * c]).astype(_BF16)
    vb_ref[...] = _dot(xb, w_ref[:, 3 * a + 2 * c:3 * a + 3 * c]).astype(_BF16)


def _rope_tables(seq):
    half = ROPE_DIMS // 2
    inv = ROPE_THETA ** (-jnp.arange(half, dtype=_F32) / half)
    ang = jnp.arange(seq, dtype=_F32)[:, None] * inv[None, :]
    cos = jnp.cos(ang)
    sin = jnp.sin(ang)
    ones = jnp.ones((seq, HEAD_DIM - ROPE_DIMS), _F32)
    zeros = jnp.zeros((seq, HEAD_DIM - ROPE_DIMS), _F32)
    zh = jnp.zeros((seq, half), _F32)
    cos_h = jnp.concatenate([cos, cos, ones], axis=1)
    sa_h = jnp.concatenate([-sin, zh, zeros], axis=1)
    sb_h = jnp.concatenate([zh, sin, zeros], axis=1)
    rep = LANES // HEAD_DIM
    return (jnp.tile(cos_h, (1, rep)), jnp.tile(sa_h, (1, rep)), jnp.tile(sb_h, (1, rep)))


def _project(x, w_qkv, tables, tm=512):
    bsz, seq, d = x.shape
    cos, sa, sb = tables
    grid = (bsz, seq // tm)
    tok = lambda w: pl.BlockSpec((None, tm, w), lambda b, t: (b, t, 0))
    tab = pl.BlockSpec((tm, LANES), lambda b, t: (t, 0))
    out_shape = [jax.ShapeDtypeStruct((bsz, seq, w), _BF16)
                 for w in (A_WIDTH, A_WIDTH, A_WIDTH, B_WIDTH, B_WIDTH, B_WIDTH)]
    return pl.pallas_call(
        _proj_kernel,
        grid=grid,
        in_specs=[tok(d), _resident(w_qkv.shape), tab, tab, tab],
        out_specs=[tok(A_WIDTH)] * 3 + [tok(B_WIDTH)] * 3,
        out_shape=out_shape,
        compiler_params=pltpu.CompilerParams(
            dimension_semantics=("parallel", "parallel"), vmem_limit_bytes=VMEM_LIMIT),
        name="proj_rope",
    )(x, w_qkv, cos, sa, sb)


def _head_select(rows_per_head, width):
    n = A_HEADS_PER_GROUP * rows_per_head
    rh = lax.broadcasted_iota(jnp.int32, (n, width), 0) // rows_per_head
    lh = lax.broadcasted_iota(jnp.int32, (n, width), 1) // HEAD_DIM
    return rh == lh


def _attn_a_kernel(mask_ref, q_ref, k_ref, v_ref, o_ref, lse_ref, *, sub_len, tile_len):
    t = pl.program_id(2)
    qb = A_QBLOCK
    nh = A_HEADS_PER_GROUP
    sel = _head_select(qb, A_GROUP_WIDTH)
    lane_head = lax.broadcasted_iota(jnp.int32, (qb, A_GROUP_WIDTH), 1) // HEAD_DIM

    def body(i, carry):
        l0 = t * tile_len + i * qb
        ks = pl.multiple_of(jnp.clip(l0 - A_SIDE, 0, sub_len - A_KWIN), A_SIDE)
        case = jnp.where(l0 == 0, 0, jnp.where(l0 == sub_len - qb, 2, 1))
        q = q_ref[pl.ds(pl.multiple_of(i * qb, qb), qb), :]
        qexp = jnp.where(sel, jnp.concatenate([q] * nh, axis=0), jnp.zeros((), _BF16))
        kw = k_ref[pl.ds(ks, A_KWIN), :]
        vw = v_ref[pl.ds(ks, A_KWIN), :]
        mask = mask_ref[case]
        s = _dot_nt(qexp, kw) + jnp.concatenate([mask] * nh, axis=0)
        m = jnp.max(s, axis=-1, keepdims=True)
        e = jnp.exp(s - m)
        den = jnp.sum(e, axis=-1, keepdims=True)
        pv = _dot(e.astype(_BF16), vw) * (1.0 / den)
        lse = m + jnp.log(den)
        o = jnp.zeros((qb, A_GROUP_WIDTH), _F32)
        lo = jnp.zeros((qb, A_GROUP_WIDTH), _F32)
        for h in range(nh):
            hit = lane_head == h
            o = jnp.where(hit, pv[h * qb:(h + 1) * qb, :], o)
            lo = jnp.where(hit, lse[h * qb:(h + 1) * qb, :], lo)
        rows = pl.ds(pl.multiple_of(i * qb, qb), qb)
        o_ref[rows, :] = o
        lse_ref[rows, :] = lo
        return carry

    lax.fori_loop(0, tile_len // qb, body, 0)


def _a_masks():
    qi = np.arange(A_QBLOCK)[:, None]
    kj = np.arange(A_KWIN)[None, :]
    out = []
    for off in (0, -A_SIDE, -2 * A_SIDE):
        ok = np.abs(kj + off - qi) <= A_SIDE
        out.append(np.where(ok, 0.0, NEG_INF))
    return jnp.asarray(np.stack(out), _F32)


def _mixer_a_group(qa, ka, va, masks, g, dilation):
    bsz, seq, _ = qa.shape
    sub_len = seq // dilation
    tile_len = min(sub_len, 1024)
    ngroups = len(A_GROUPS)
    view = lambda t: t.reshape(bsz, sub_len, dilation * A_WIDTH)
    grid = (bsz, dilation, sub_len // tile_len)
    col = lambda r: r * ngroups + g
    q_spec = pl.BlockSpec((None, tile_len, A_GROUP_WIDTH), lambda b, r, t: (b, t, col(r)))
    kv_spec = pl.BlockSpec((None, sub_len, A_GROUP_WIDTH), lambda b, r, t: (b, 0, col(r)))
    o_spec = pl.BlockSpec((None, tile_len, A_GROUP_WIDTH), lambda b, r, t: (b, t, r))
    out_sds = jax.ShapeDtypeStruct((bsz, sub_len, dilation * A_GROUP_WIDTH), _F32)
    o, lse = pl.pallas_call(
        functools.partial(_attn_a_kernel, sub_len=sub_len, tile_len=tile_len),
        grid=grid,
        in_specs=[_resident(masks.shape), q_spec, kv_spec, kv_spec],
        out_specs=[o_spec, o_spec],
        out_shape=[out_sds, out_sds],
        compiler_params=pltpu.CompilerParams(
            dimension_semantics=("parallel", "parallel", "arbitrary"), vmem_limit_bytes=VMEM_LIMIT),
        name=f"mixer_a_d{dilation}",
    )(masks, view(qa), view(ka), view(va))
    return o.reshape(bsz, seq, A_GROUP_WIDTH), lse.reshape(bsz, seq, A_GROUP_WIDTH)


def _na_col_start(j):
    return min(max(j * NA_QCOLS - NA_COLS // 2, 0), GRID_W - NA_KCOLS)


def _na_bias_table(rpb, rows):
    nrb = rows // NA_QROWS
    qr = np.arange(NA_QROWS)
    qc = np.arange(NA_QCOLS)
    kr = np.arange(NA_KROWS)
    kc = np.arange(NA_KCOLS)
    dr_all, okr_all = [], []
    for i in (0, 1, nrb - 1):
        r_abs = i * NA_QROWS + qr
        win_r = np.clip(r_abs - NA_ROWS // 2, 0, rows - NA_ROWS)
        k_abs = min(max(i * NA_QROWS - NA_ROWS // 2, 0), rows - NA_KROWS) + kr
        okr_all.append((k_abs[None, :] >= win_r[:, None]) & (k_abs[None, :] < win_r[:, None] + NA_ROWS))
        dr_all.append(np.clip(k_abs[None, :] - r_abs[:, None] + NA_ROWS - 1, 0, 2 * NA_ROWS - 2))
    dc_all, okc_all = [], []
    for j in range(GRID_W // NA_QCOLS):
        c_abs = j * NA_QCOLS + qc
        win_c = np.clip(c_abs - NA_COLS // 2, 0, GRID_W - NA_COLS)
        k_abs = _na_col_start(j) + kc
        okc_all.append((k_abs[None, :] >= win_c[:, None]) & (k_abs[None, :] < win_c[:, None] + NA_COLS))
        dc_all.append(np.clip(k_abs[None, :] - c_abs[:, None] + NA_COLS - 1, 0, 2 * NA_COLS - 2))
    dr = np.stack(dr_all)
    okr = np.stack(okr_all)
    dc = np.stack(dc_all)
    okc = np.stack(okc_all)
    dr6 = dr[:, None, :, None, :, None]
    dc6 = dc[None, :, None, :, None, :]
    ok6 = okr[:, None, :, None, :, None] & okc[None, :, None, :, None, :]
    dr6, dc6 = np.broadcast_arrays(dr6, dc6)
    bias = rpb.astype(_F32)[:, dr6, dc6]
    bias = jnp.where(jnp.asarray(ok6)[None], bias, NEG_INF)
    nq = NA_QROWS * NA_QCOLS
    nk = NA_KROWS * NA_KCOLS
    return bias.reshape(B_HEADS, 3, GRID_W // NA_QCOLS, nq, nk).transpose(1, 2, 0, 3, 4)


def _attn_b_kernel(bias_ref, q_ref, k0_ref, k1_ref, k2_ref, k3_ref,
                   v0_ref, v1_ref, v2_ref, v3_ref, o_ref):
    k_refs = (k0_ref, k1_ref, k2_ref, k3_ref)
    v_refs = (v0_ref, v1_ref, v2_ref, v3_ref)
    nq = NA_QROWS * NA_QCOLS
    nh = A_HEADS_PER_GROUP
    half_w = nh * HEAD_DIM
    sel = _head_select(nq, half_w)
    lane_head = lax.broadcasted_iota(jnp.int32, (nq, half_w), 1) // HEAD_DIM

    for j in range(GRID_W // NA_QCOLS):
        cs = _na_col_start(j)
        for hh in range(B_HEADS // nh):
            lanes = slice(hh * half_w, (hh + 1) * half_w)
            q = jnp.concatenate(
                [q_ref[r * GRID_W + j * NA_QCOLS:r * GRID_W + (j + 1) * NA_QCOLS, lanes]
                 for r in range(NA_QROWS)], axis=0)

            def window(refs):
                return jnp.concatenate(
                    [refs[kr // NA_PIECE_ROWS][(kr % NA_PIECE_ROWS) * GRID_W + cs:
                                               (kr % NA_PIECE_ROWS) * GRID_W + cs + NA_KCOLS, lanes]
                     for kr in range(NA_KROWS)], axis=0)

            kw = window(k_refs)
            vw = window(v_refs)
            qexp = jnp.where(sel, jnp.concatenate([q] * nh, axis=0), jnp.zeros((), _BF16))
            bias = jnp.concatenate([bias_ref[j, hh * nh + h] for h in range(nh)], axis=0)
            s = _dot_nt(qexp, kw) + bias
            m = jnp.max(s, axis=-1, keepdims=True)
            e = jnp.exp(s - m)
            den = jnp.sum(e, axis=-1, keepdims=True)
            pv = _dot(e.astype(_BF16), vw) * (1.0 / den)
            o = jnp.zeros((nq, half_w), _F32)
            for h in range(nh):
                o = jnp.where(lane_head == h, pv[h * nq:(h + 1) * nq, :], o)
            ob = o.astype(_BF16)
            for r in range(NA_QROWS):
                o_ref[r * GRID_W + j * NA_QCOLS:r * GRID_W + (j + 1) * NA_QCOLS, lanes] = (
                    ob[r * NA_QCOLS:(r + 1) * NA_QCOLS, :])


def _mixer_b(qb, kb, vb, bias_table):
    bsz, seq, _ = qb.shape
    rows = seq // GRID_W
    nrb = rows // NA_QROWS
    q_tokens = NA_QROWS * GRID_W
    piece_tokens = NA_PIECE_ROWS * GRID_W
    npieces = NA_KROWS // NA_PIECE_ROWS
    last_piece = rows // NA_PIECE_ROWS - npieces

    def piece0(i):
        return jnp.clip(i * (NA_QROWS // NA_PIECE_ROWS) - 1, 0, last_piece)

    def row_case(i):
        return jnp.where(i == 0, 0, jnp.where(i == nrb - 1, 2, 1))

    bias_spec = pl.BlockSpec((None,) + bias_table.shape[1:], lambda b, i: (row_case(i), 0, 0, 0, 0))
    q_spec = pl.BlockSpec((None, q_tokens, B_WIDTH), lambda b, i: (b, i, 0))
    piece_specs = [pl.BlockSpec((None, piece_tokens, B_WIDTH), lambda b, i, n=n: (b, piece0(i) + n, 0))
                   for n in range(npieces)]
    return pl.pallas_call(
        _attn_b_kernel,
        grid=(bsz, nrb),
        in_specs=[bias_spec, q_spec] + piece_specs + piece_specs,
        out_specs=q_spec,
        out_shape=jax.ShapeDtypeStruct((bsz, seq, B_WIDTH), _BF16),
        compiler_params=pltpu.CompilerParams(
            dimension_semantics=("parallel", "arbitrary"), vmem_limit_bytes=VMEM_LIMIT),
        name="mixer_b",
    )(bias_table, qb, kb, kb, kb, kb, vb, vb, vb, vb)


def _layer_norm(h, g, b):
    mu = jnp.mean(h, axis=-1, keepdims=True)
    c = h - mu
    var = jnp.mean(c * c, axis=-1, keepdims=True)
    return c * lax.rsqrt(var + LN_EPS) * g + b


def _post_kernel(x_ref, o1_ref, o2_ref, o3_ref, l1_ref, l2_ref, l3_ref, yb_ref,
                 wg_ref, bg_ref, wa_ref, wb_ref, wo_ref, ln1g_ref, ln1b_ref,
                 w1_ref, b1_ref, w2_ref, b2_ref, ln2g_ref, ln2b_ref, y_ref, *, alpha):
    x = x_ref[...]
    xb = x.astype(_BF16)
    l1 = l1_ref[...]
    l2 = l2_ref[...]
    l3 = l3_ref[...]
    lm = jnp.maximum(jnp.maximum(l1, l2), l3)
    e1 = jnp.exp(l1 - lm)
    e2 = jnp.exp(l2 - lm)
    e3 = jnp.exp(l3 - lm)
    ya = (e1 * o1_ref[...] + e2 * o2_ref[...] + e3 * o3_ref[...]) * (1.0 / (e1 + e2 + e3))
    ya = _dot(ya.astype(_BF16), wa_ref[...])
    yb = _dot(yb_ref[...], wb_ref[...])
    gates = _dot(xb, wg_ref[...])
    ga = jax.nn.sigmoid(gates[:, :D_MODEL] + bg_ref[0:1, :])
    gb = jax.nn.sigmoid(gates[:, D_MODEL:] + bg_ref[1:2, :])
    merged = ga * ya + gb * yb
    h = alpha * x + _dot(merged.astype(_BF16), wo_ref[...])
    x1 = _layer_norm(h, ln1g_ref[...], ln1b_ref[...])
    hid = jnp.maximum(_dot(x1.astype(_BF16), w1_ref[...]) + b1_ref[...], 0.0)
    hid = hid * hid
    h2 = alpha * x1 + _dot(hid.astype(_BF16), w2_ref[...]) + b2_ref[...]
    y_ref[...] = _layer_norm(h2, ln2g_ref[...], ln2b_ref[...])


def _post(x, a_outs, a_lses, yb, p, alpha, tm=256):
    bsz, seq, d = x.shape
    tok = lambda w: pl.BlockSpec((None, tm, w), lambda b, t: (b, t, 0))
    weights = (p["w_gate"], p["b_gate"], p["w_a"], p["w_b"], p["w_o"], p["ln1_g"], p["ln1_b"],
               p["w_ff1"], p["b_ff1"], p["w_ff2"], p["b_ff2"], p["ln2_g"], p["ln2_b"])
    return pl.pallas_call(
        functools.partial(_post_kernel, alpha=alpha),
        grid=(bsz, seq // tm),
        in_specs=[tok(d)] + [tok(A_GROUP_WIDTH)] * 6 + [tok(B_WIDTH)] + [_resident(w.shape) for w in weights],
        out_specs=tok(d),
        out_shape=jax.ShapeDtypeStruct((bsz, seq, d), _F32),
        compiler_params=pltpu.CompilerParams(
            dimension_semantics=("parallel", "parallel"), vmem_limit_bytes=VMEM_LIMIT),
        name="post_ffn",
    )(x, *a_outs, *a_lses, yb, *weights)


def _encoder_layer(x, p, consts, alpha):
    qa, ka, va, qb, kb, vb = _project(x, p["w_qkv"], consts["rope"])
    a_outs, a_lses = [], []
    for g, (_, dilation) in enumerate(A_GROUPS):
        o, lse = _mixer_a_group(qa, ka, va, consts["a_masks"], g, dilation)
        a_outs.append(o)
        a_lses.append(lse)
    yb = _mixer_b(qb, kb, vb, p["na_bias"])
    return _post(x, a_outs, a_lses, yb, p, alpha)


def kernel(x_prompt, x_sample, w_in, b_gate, w_branch_a, w_branch_b, w_out, rel_pos_bias,
           ln1_g, ln1_b, w_ff1, b_ff1, w_ff2, b_ff2, ln2_g, ln2_b):
    depth = w_in.shape[0]
    alpha = (2.0 * depth) ** 0.25
    seq = x_prompt.shape[1]
    assert x_sample.shape[1] == seq and seq % (GRID_W * NA_KROWS) == 0
    consts = {"rope": _rope_tables(seq), "a_masks": _a_masks()}
    scale = HEAD_DIM ** -0.5
    col_scale = np.ones((QKV_COLS,), np.float32)
    col_scale[0:A_WIDTH] = scale
    col_scale[3 * A_WIDTH:3 * A_WIDTH + B_WIDTH] = scale
    row = lambda v: v.reshape(1, -1)
    y_prompt, y_sample = x_prompt, x_sample
    for layer in range(depth):
        p = {
            "w_qkv": (w_in[layer][:, :QKV_COLS] * col_scale).astype(_BF16),
            "w_gate": w_in[layer][:, QKV_COLS:].astype(_BF16),
            "b_gate": b_gate[layer],
            "w_a": w_branch_a[layer].astype(_BF16),
            "w_b": w_branch_b[layer].astype(_BF16),
            "w_o": w_out[layer].astype(_BF16),
            "na_bias": _na_bias_table(rel_pos_bias[layer], seq // GRID_W),
            "ln1_g": row(ln1_g[layer]), "ln1_b": row(ln1_b[layer]),
            "w_ff1": w_ff1[layer].astype(_BF16), "b_ff1": row(b_ff1[layer]),
            "w_ff2": w_ff2[layer].astype(_BF16), "b_ff2": row(b_ff2[layer]),
            "ln2_g": row(ln2_g[layer]), "ln2_b": row(ln2_b[layer]),
        }
        y_prompt = _encoder_layer(y_prompt, p, consts, alpha)
        y_sample = _encoder_layer(y_sample, p, consts, alpha)
    return (y_prompt, y_sample)
```

```python
import functools
import math

import numpy as np
import jax
import jax.numpy as jnp
from jax import lax
from jax.experimental import pallas as pl
from jax.experimental.pallas import tpu as pltpu

D_MODEL = 1024
HEAD_DIM = 64
A_GROUPS = ((128, 1), (512, 4), (2048, 16))
A_DILATIONS = tuple(d for _, d in A_GROUPS)
A_HEADS_PER_GROUP = 4
A_GROUP_WIDTH = A_HEADS_PER_GROUP * HEAD_DIM
A_WIDTH = len(A_GROUPS) * A_GROUP_WIDTH
A_SIDE = 64
A_QBLOCK = 128
A_KWIN = A_QBLOCK + 2 * A_SIDE
A_TILE = A_QBLOCK * max(A_DILATIONS)
B_HEADS = 8
B_WIDTH = B_HEADS * HEAD_DIM
GRID_W = 64
NA_ROWS = 8
NA_COLS = 16
NA_QROWS = 8
NA_QCOLS = 16
NA_KROWS = 16
NA_KCOLS = 32
NA_PIECE_ROWS = 4
ROPE_THETA = 500000.0
ROPE_DIMS = HEAD_DIM // 4
D_FF = 4 * D_MODEL
FF_CHUNK = 1024
QKV_COLS = 3 * A_WIDTH + 3 * B_WIDTH
LN_EPS = 1e-5
NEG_INF = -1e30
LOG2E = math.log2(math.e)
LANES = 128

VMEM_LIMIT = 56 * 1024 * 1024

_BF16 = jnp.bfloat16
_F32 = jnp.float32


def _dot(a, b):
    return jnp.dot(a, b, preferred_element_type=_F32)


def _dot_nt(a, b):
    return lax.dot_general(a, b, (((1,), (1,)), ((), ())), preferred_element_type=_F32)


def _resident(shape):
    nd = len(shape)
    return pl.BlockSpec(shape, lambda *_: (0,) * nd, pipeline_mode=pl.Buffered(1))


def _head_select(rows_per_head, width):
    n = A_HEADS_PER_GROUP * rows_per_head
    rh = lax.broadcasted_iota(jnp.int32, (n, width), 0) // rows_per_head
    lh = lax.broadcasted_iota(jnp.int32, (n, width), 1) // HEAD_DIM
    return rh == lh


def _diag_blocks(t, nq, left):
    cols = (lambda a, c: a) if t.shape[1] == 1 else (lambda a, c: a[:, c * LANES:(c + 1) * LANES])
    halves = [jnp.where(left, cols(t[(2 * c) * nq:(2 * c + 1) * nq], c), cols(t[(2 * c + 1) * nq:(2 * c + 2) * nq], c))
              for c in range(A_GROUP_WIDTH // LANES)]
    return jnp.concatenate(halves, axis=1)


def _softmax_pv(qexp, kw, vw, bias):
    s = _dot_nt(qexp, kw) + bias
    m = jnp.max(s, axis=-1, keepdims=True)
    e = jnp.exp2(s - m)
    den = jnp.sum(e, axis=-1, keepdims=True)
    return _dot(e.astype(_BF16), vw), m, den


def _proj_kernel(x_ref, w_ref, cos_ref, sa_ref, sb_ref,
                 q1_ref, k1_ref, v1_ref, q2_ref, k2_ref, v2_ref, q3_ref, k3_ref, v3_ref,
                 qb_ref, kb_ref, vb_ref, scr_ref):
    tm = x_ref.shape[0]
    xb = x_ref[...].astype(_BF16)
    cos = cos_ref[...]
    sa = sa_ref[...]
    sb = sb_ref[...]

    def rope(acc):
        outs = []
        for c in range(A_WIDTH // LANES):
            ch = acc[:, c * LANES:(c + 1) * LANES]
            up = pltpu.roll(ch, LANES - ROPE_DIMS // 2, 1)
            dn = pltpu.roll(ch, ROPE_DIMS // 2, 1)
            outs.append(ch * cos + up * sa + dn * sb)
        return jnp.concatenate(outs, axis=1)

    def emit(acc, outs, slot):
        gw = A_GROUP_WIDTH
        outs[0][...] = acc[:, 0:gw].astype(_BF16)
        for g in (1, 2):
            d = A_DILATIONS[g]
            for c in range(gw // LANES):
                buf = scr_ref.at[(2 * slot + g - 1) * (gw // LANES) + c]
                buf[...] = acc[:, g * gw + c * LANES:g * gw + (c + 1) * LANES]
                for r in range(d):
                    outs[g][r, :, c * LANES:(c + 1) * LANES] = buf[pl.ds(r, tm // d, stride=d), :].astype(_BF16)

    a = A_WIDTH
    c = B_WIDTH
    emit(rope(_dot(xb, w_ref[:, 0:a])), (q1_ref, q2_ref, q3_ref), 0)
    emit(rope(_dot(xb, w_ref[:, a:2 * a])), (k1_ref, k2_ref, k3_ref), 1)
    emit(_dot(xb, w_ref[:, 2 * a:3 * a]), (v1_ref, v2_ref, v3_ref), 2)
    qb_ref[...] = _dot(xb, w_ref[:, 3 * a:3 * a + c]).astype(_BF16)
    kb_ref[...] = _dot(xb, w_ref[:, 3 * a + c:3 * a + 2 * c]).astype(_BF16)
    vb_ref[...] = _dot(xb, w_ref[:, 3 * a + 2 * c:3 * a + 3 * c]).astype(_BF16)


def _rope_tables(seq):
    half = ROPE_DIMS // 2
    inv = ROPE_THETA ** (-jnp.arange(half, dtype=_F32) / half)
    ang = jnp.arange(seq, dtype=_F32)[:, None] * inv[None, :]
    cos = jnp.cos(ang)
    sin = jnp.sin(ang)
    ones = jnp.ones((seq, HEAD_DIM - ROPE_DIMS), _F32)
    zeros = jnp.zeros((seq, HEAD_DIM - ROPE_DIMS), _F32)
    zh = jnp.zeros((seq, half), _F32)
    cos_h = jnp.concatenate([cos, cos, ones], axis=1)
    sa_h = jnp.concatenate([-sin, zh, zeros], axis=1)
    sb_h = jnp.concatenate([zh, sin, zeros], axis=1)
    rep = LANES // HEAD_DIM
    return (jnp.tile(cos_h, (1, rep)), jnp.tile(sa_h, (1, rep)), jnp.tile(sb_h, (1, rep)))


def _project(x, w_qkv, tables, tm=512):
    bsz, seq, d = x.shape
    cos, sa, sb = tables
    grid = (bsz, seq // tm)
    tok = lambda w: pl.BlockSpec((None, tm, w), lambda b, t: (b, t, 0))
    tab = pl.BlockSpec((tm, LANES), lambda b, t: (t, 0))
    gw = A_GROUP_WIDTH
    a_specs, a_shapes = [], []
    for dil in A_DILATIONS:
        if dil == 1:
            a_specs.append(tok(gw))
            a_shapes.append(jax.ShapeDtypeStruct((bsz, seq, gw), _BF16))
        else:
            a_specs.append(pl.BlockSpec((None, dil, tm // dil, gw), lambda b, t: (b, 0, t, 0)))
            a_shapes.append(jax.ShapeDtypeStruct((bsz, dil, seq // dil, gw), _BF16))
    b_shape = jax.ShapeDtypeStruct((bsz, seq, B_WIDTH), _BF16)
    out_specs = [a_specs[g] for g in range(3) for _ in range(3)] + [tok(B_WIDTH)] * 3
    out_shape = [a_shapes[g] for g in range(3) for _ in range(3)] + [b_shape] * 3
    return pl.pallas_call(
        _proj_kernel,
        grid=grid,
        in_specs=[tok(d), _resident(w_qkv.shape), tab, tab, tab],
        out_specs=out_specs,
        out_shape=out_shape,
        scratch_shapes=[pltpu.VMEM((6 * gw // LANES, tm, LANES), _F32)],
        compiler_params=pltpu.CompilerParams(
            dimension_semantics=("parallel", "parallel"), vmem_limit_bytes=VMEM_LIMIT),
        name="proj_rope",
    )(x, w_qkv, cos, sa, sb)


def _mixer_a_kernel(mask_ref, q1_ref, q2_ref, q3_ref, k1_ref, k2_ref, k3_ref, v1_ref, v2_ref, v3_ref,
                    y_ref, o2_scr, l2_scr, o3_scr, l3_scr, *, seq):
    t = pl.program_id(1)
    qb = A_QBLOCK
    nh = A_HEADS_PER_GROUP
    sel = _head_select(qb, A_GROUP_WIDTH)
    left = lax.broadcasted_iota(jnp.int32, (qb, LANES), 1) < HEAD_DIM

    def block(q_ref, k_ref, v_ref, q_row0, l0, sub_len):
        ks = pl.multiple_of(jnp.clip(l0 - A_SIDE, 0, sub_len - A_KWIN), A_SIDE)
        case = jnp.where(l0 == 0, 0, jnp.where(l0 == sub_len - qb, 2, 1))
        if not isinstance(q_row0, int):
            q_row0 = pl.multiple_of(q_row0, qb)
        q = q_ref[pl.ds(q_row0, qb), :]
        qexp = jnp.where(sel, jnp.concatenate([q] * nh, axis=0), jnp.zeros((), _BF16))
        pv, m, den = _softmax_pv(qexp, k_ref[pl.ds(ks, A_KWIN), :], v_ref[pl.ds(ks, A_KWIN), :], mask_ref[case])
        o = _diag_blocks(pv, qb, left) * _diag_blocks(1.0 / den, qb, left)
        return o, _diag_blocks(m + jnp.log2(den), qb, left)

    nhalf = A_GROUP_WIDTH // LANES

    def scatter(scr, start, stride, val):
        for c in range(nhalf):
            scr[c, pl.ds(start, qb, stride=stride), :] = val[:, c * LANES:(c + 1) * LANES]

    def gather(scr, rows):
        return jnp.concatenate([scr[c, rows, :] for c in range(nhalf)], axis=1)

    d3 = A_DILATIONS[2]
    rows3 = A_TILE // d3

    def group3(r, carry):
        o, l = block(q3_ref.at[r], k3_ref.at[r], v3_ref.at[r], 0, t * rows3, seq // d3)
        scatter(o3_scr, r, d3, o)
        scatter(l3_scr, r, d3, l)
        return carry

    lax.fori_loop(0, d3, group3, 0, unroll=2)

    d2 = A_DILATIONS[1]
    rows2 = A_TILE // d2
    nblk2 = rows2 // qb

    def group2(n, carry):
        r = n // nblk2
        i = n % nblk2
        o, l = block(q2_ref.at[r], k2_ref.at[r], v2_ref.at[r], i * qb, t * rows2 + i * qb, seq // d2)
        scatter(o2_scr, i * qb * d2 + r, d2, o)
        scatter(l2_scr, i * qb * d2 + r, d2, l)
        return carry

    lax.fori_loop(0, d2 * nblk2, group2, 0, unroll=2)

    def group1(i, carry):
        o1, l1 = block(q1_ref, k1_ref, v1_ref, i * qb, t * A_TILE + i * qb, seq)
        rows = pl.ds(pl.multiple_of(i * qb, qb), qb)
        l2 = gather(l2_scr, rows)
        l3 = gather(l3_scr, rows)
        lm = jnp.maximum(jnp.maximum(l1, l2), l3)
        e1 = jnp.exp2(l1 - lm)
        e2 = jnp.exp2(l2 - lm)
        e3 = jnp.exp2(l3 - lm)
        ya = (e1 * o1 + e2 * gather(o2_scr, rows) + e3 * gather(o3_scr, rows)) * (1.0 / (e1 + e2 + e3))
        y_ref[rows, :] = ya.astype(_BF16)
        return carry

    lax.fori_loop(0, A_TILE // qb, group1, 0, unroll=2)


def _a_masks():
    qi = np.arange(A_QBLOCK)[:, None]
    kj = np.arange(A_KWIN)[None, :]
    out = []
    for off in (0, -A_SIDE, -2 * A_SIDE):
        ok = np.abs(kj + off - qi) <= A_SIDE
        out.append(np.tile(np.where(ok, 0.0, NEG_INF), (A_HEADS_PER_GROUP, 1)))
    return jnp.asarray(np.stack(out), _F32)


def _mixer_a(qkv, masks):
    q1, k1, v1, q2, k2, v2, q3, k3, v3 = qkv
    bsz, seq, gw = q1.shape
    assert seq % A_TILE == 0 and seq // max(A_DILATIONS) >= A_KWIN

    def q_spec(dil):
        if dil == 1:
            return pl.BlockSpec((None, A_TILE, gw), lambda b, t: (b, t, 0))
        return pl.BlockSpec((None, dil, A_TILE // dil, gw), lambda b, t: (b, 0, t, 0))

    def kv_spec(dil):
        if dil == 1:
            return pl.BlockSpec((None, seq, gw), lambda b, t: (b, 0, 0), pipeline_mode=pl.Buffered(1))
        return pl.BlockSpec((None, dil, seq // dil, gw), lambda b, t: (b, 0, 0, 0), pipeline_mode=pl.Buffered(1))

    qs = [q_spec(d) for d in A_DILATIONS]
    kvs = [kv_spec(d) for d in A_DILATIONS]
    return pl.pallas_call(
        functools.partial(_mixer_a_kernel, seq=seq),
        grid=(bsz, seq // A_TILE),
        in_specs=[_resident(masks.shape)] + qs + kvs + kvs,
        out_specs=pl.BlockSpec((None, A_TILE, gw), lambda b, t: (b, t, 0)),
        out_shape=jax.ShapeDtypeStruct((bsz, seq, gw), _BF16),
        scratch_shapes=[pltpu.VMEM((gw // LANES, A_TILE, LANES), _F32)] * 4,
        compiler_params=pltpu.CompilerParams(
            dimension_semantics=("parallel", "arbitrary"), vmem_limit_bytes=VMEM_LIMIT),
        name="mixer_a",
    )(masks, q1, q2, q3, k1, k2, k3, v1, v2, v3)


def _na_col_start(j):
    return min(max(j * NA_QCOLS - NA_COLS // 2, 0), GRID_W - NA_KCOLS)


def _na_bias_table(rpb, rows):
    nrb = rows // NA_QROWS
    qr = np.arange(NA_QROWS)
    qc = np.arange(NA_QCOLS)
    kr = np.arange(NA_KROWS)
    kc = np.arange(NA_KCOLS)
    dr_all, okr_all = [], []
    for i in (0, 1, nrb - 1):
        r_abs = i * NA_QROWS + qr
        win_r = np.clip(r_abs - NA_ROWS // 2, 0, rows - NA_ROWS)
        k_abs = min(max(i * NA_QROWS - NA_ROWS // 2, 0), rows - NA_KROWS) + kr
        okr_all.append((k_abs[None, :] >= win_r[:, None]) & (k_abs[None, :] < win_r[:, None] + NA_ROWS))
        dr_all.append(np.clip(k_abs[None, :] - r_abs[:, None] + NA_ROWS - 1, 0, 2 * NA_ROWS - 2))
    dc_all, okc_all = [], []
    for j in range(GRID_W // NA_QCOLS):
        c_abs = j * NA_QCOLS + qc
        win_c = np.clip(c_abs - NA_COLS // 2, 0, GRID_W - NA_COLS)
        k_abs = _na_col_start(j) + kc
        okc_all.append((k_abs[None, :] >= win_c[:, None]) & (k_abs[None, :] < win_c[:, None] + NA_COLS))
        dc_all.append(np.clip(k_abs[None, :] - c_abs[:, None] + NA_COLS - 1, 0, 2 * NA_COLS - 2))
    dr = np.stack(dr_all)
    okr = np.stack(okr_all)
    dc = np.stack(dc_all)
    okc = np.stack(okc_all)
    rsel = jnp.asarray(np.eye(2 * NA_ROWS - 1, dtype=np.float32)[dr])
    csel = jnp.asarray(np.eye(2 * NA_COLS - 1, dtype=np.float32)[dc])
    hi = lax.Precision.HIGHEST
    t1 = jnp.einsum("hab,cqkb->hacqk", rpb.astype(_F32), csel, precision=hi)
    bias = jnp.einsum("rpsa,hacqk->rchpqsk", rsel, t1, precision=hi)
    ok = okr[:, None, None, :, None, :, None] & okc[None, :, None, None, :, None, :]
    bias = jnp.where(jnp.asarray(ok), bias * LOG2E, NEG_INF)
    return bias.reshape(3, GRID_W // NA_QCOLS, B_HEADS, NA_QROWS * NA_QCOLS, NA_KROWS * NA_KCOLS)


def _attn_b_kernel(bias_ref, q_ref, k0_ref, k1_ref, k2_ref, k3_ref,
                   v0_ref, v1_ref, v2_ref, v3_ref, o_ref):
    k_refs = (k0_ref, k1_ref, k2_ref, k3_ref)
    v_refs = (v0_ref, v1_ref, v2_ref, v3_ref)
    nq = NA_QROWS * NA_QCOLS
    nh = A_HEADS_PER_GROUP
    half_w = nh * HEAD_DIM
    sel = _head_select(nq, half_w)
    left = lax.broadcasted_iota(jnp.int32, (nq, LANES), 1) < HEAD_DIM

    for j in range(GRID_W // NA_QCOLS):
        cs = _na_col_start(j)
        for hh in range(B_HEADS // nh):
            lanes = slice(hh * half_w, (hh + 1) * half_w)
            q = jnp.concatenate(
                [q_ref[r * GRID_W + j * NA_QCOLS:r * GRID_W + (j + 1) * NA_QCOLS, lanes]
                 for r in range(NA_QROWS)], axis=0)

            def window(refs):
                return jnp.concatenate(
                    [refs[kr // NA_PIECE_ROWS][(kr % NA_PIECE_ROWS) * GRID_W + cs:
                                               (kr % NA_PIECE_ROWS) * GRID_W + cs + NA_KCOLS, lanes]
                     for kr in range(NA_KROWS)], axis=0)

            qexp = jnp.where(sel, jnp.concatenate([q] * nh, axis=0), jnp.zeros((), _BF16))
            bias = bias_ref[j, hh * nh:(hh + 1) * nh].reshape(nh * nq, NA_KROWS * NA_KCOLS)
            pv, _, den = _softmax_pv(qexp, window(k_refs), window(v_refs), bias)
            ob = (_diag_blocks(pv, nq, left) * _diag_blocks(1.0 / den, nq, left)).astype(_BF16)
            for r in range(NA_QROWS):
                o_ref[r * GRID_W + j * NA_QCOLS:r * GRID_W + (j + 1) * NA_QCOLS, lanes] = (
                    ob[r * NA_QCOLS:(r + 1) * NA_QCOLS, :])


def _mixer_b(qb, kb, vb, bias_table):
    bsz, seq, _ = qb.shape
    rows = seq // GRID_W
    nrb = rows // NA_QROWS
    q_tokens = NA_QROWS * GRID_W
    piece_tokens = NA_PIECE_ROWS * GRID_W
    npieces = NA_KROWS // NA_PIECE_ROWS
    last_piece = rows // NA_PIECE_ROWS - npieces

    def piece0(i):
        return jnp.clip(i * (NA_QROWS // NA_PIECE_ROWS) - 1, 0, last_piece)

    def row_case(i):
        return jnp.where(i == 0, 0, jnp.where(i == nrb - 1, 2, 1))

    bias_spec = pl.BlockSpec((None,) + bias_table.shape[1:], lambda b, i: (row_case(i), 0, 0, 0, 0))
    q_spec = pl.BlockSpec((None, q_tokens, B_WIDTH), lambda b, i: (b, i, 0))
    piece_specs = [pl.BlockSpec((None, piece_tokens, B_WIDTH), lambda b, i, n=n: (b, piece0(i) + n, 0))
                   for n in range(npieces)]
    return pl.pallas_call(
        _attn_b_kernel,
        grid=(bsz, nrb),
        in_specs=[bias_spec, q_spec] + piece_specs + piece_specs,
        out_specs=q_spec,
        out_shape=jax.ShapeDtypeStruct((bsz, seq, B_WIDTH), _BF16),
        compiler_params=pltpu.CompilerParams(
            dimension_semantics=("parallel", "arbitrary"), vmem_limit_bytes=VMEM_LIMIT),
        name="mixer_b",
    )(bias_table, qb, kb, kb, kb, kb, vb, vb, vb, vb)


def _layer_norm(h, g, b):
    mu = jnp.mean(h, axis=-1, keepdims=True)
    c = h - mu
    var = jnp.mean(c * c, axis=-1, keepdims=True)
    return c * lax.rsqrt(var + LN_EPS) * g + b


def _post_kernel(x_ref, ya_ref, yb_ref,
                 wg_ref, bg_ref, wa_ref, wb_ref, wo_ref, ln1g_ref, ln1b_ref,
                 w1_ref, b1_ref, w2_ref, b2_ref, ln2g_ref, ln2b_ref, y_ref, *, alpha):
    x = x_ref[...]
    xb = x.astype(_BF16)
    ya = _dot(ya_ref[...], wa_ref[...])
    yb = _dot(yb_ref[...], wb_ref[...])
    ga = jax.nn.sigmoid(_dot(xb, wg_ref[:, :D_MODEL]) + bg_ref[0:1, :])
    gb = jax.nn.sigmoid(_dot(xb, wg_ref[:, D_MODEL:]) + bg_ref[1:2, :])
    merged = ga * ya + gb * yb
    h = alpha * x + _dot(merged.astype(_BF16), wo_ref[...])
    x1 = _layer_norm(h, ln1g_ref[...], ln1b_ref[...])
    x1b = x1.astype(_BF16)
    h2 = alpha * x1 + b2_ref[...]
    for c in range(D_FF // FF_CHUNK):
        cols = slice(c * FF_CHUNK, (c + 1) * FF_CHUNK)
        hid = jnp.maximum(_dot(x1b, w1_ref[:, cols]) + b1_ref[:, cols], 0.0)
        h2 = h2 + _dot((hid * hid).astype(_BF16), w2_ref[cols, :])
    y_ref[...] = _layer_norm(h2, ln2g_ref[...], ln2b_ref[...])


def _post(x, ya, yb, p, alpha, tm=512):
    bsz, seq, d = x.shape
    tok = lambda w: pl.BlockSpec((None, tm, w), lambda b, t: (b, t, 0))
    weights = (p["w_gate"], p["b_gate"], p["w_a"], p["w_b"], p["w_o"], p["ln1_g"], p["ln1_b"],
               p["w_ff1"], p["b_ff1"], p["w_ff2"], p["b_ff2"], p["ln2_g"], p["ln2_b"])
    return pl.pallas_call(
        functools.partial(_post_kernel, alpha=alpha),
        grid=(bsz, seq // tm),
        in_specs=[tok(d), tok(A_GROUP_WIDTH), tok(B_WIDTH)] + [_resident(w.shape) for w in weights],
        out_specs=tok(d),
        out_shape=jax.ShapeDtypeStruct((bsz, seq, d), _F32),
        compiler_params=pltpu.CompilerParams(
            dimension_semantics=("parallel", "parallel"), vmem_limit_bytes=VMEM_LIMIT),
        name="post_ffn",
    )(x, ya, yb, *weights)


def _encoder_layer(x, p, consts, alpha):
    outs = _project(x, p["w_qkv"], consts["rope"])
    ya = _mixer_a(outs[:9], consts["a_masks"])
    yb = _mixer_b(*outs[9:], p["na_bias"])
    return _post(x, ya, yb, p, alpha)


def kernel(x_prompt, x_sample, w_in, b_gate, w_branch_a, w_branch_b, w_out, rel_pos_bias,
           ln1_g, ln1_b, w_ff1, b_ff1, w_ff2, b_ff2, ln2_g, ln2_b):
    depth = w_in.shape[0]
    alpha = (2.0 * depth) ** 0.25
    seq = x_prompt.shape[1]
    assert x_sample.shape[1] == seq and seq % (GRID_W * NA_KROWS) == 0
    consts = {"rope": _rope_tables(seq), "a_masks": _a_masks()}
    scale = HEAD_DIM ** -0.5 * LOG2E
    col_scale = np.ones((QKV_COLS,), np.float32)
    col_scale[0:A_WIDTH] = scale
    col_scale[3 * A_WIDTH:3 * A_WIDTH + B_WIDTH] = scale
    row = lambda v: v.reshape(1, -1)
    y_prompt, y_sample = x_prompt, x_sample
    for layer in range(depth):
        p = {
            "w_qkv": (w_in[layer][:, :QKV_COLS] * col_scale).astype(_BF16),
            "w_gate": w_in[layer][:, QKV_COLS:].astype(_BF16),
            "b_gate": b_gate[layer],
            "w_a": w_branch_a[layer].astype(_BF16),
            "w_b": w_branch_b[layer].astype(_BF16),
            "w_o": w_out[layer].astype(_BF16),
            "na_bias": _na_bias_table(rel_pos_bias[layer], seq // GRID_W),
            "ln1_g": row(ln1_g[layer]), "ln1_b": row(ln1_b[layer]),
            "w_ff1": w_ff1[layer].astype(_BF16), "b_ff1": row(b_ff1[layer]),
            "w_ff2": w_ff2[layer].astype(_BF16), "b_ff2": row(b_ff2[layer]),
            "ln2_g": row(ln2_g[layer]), "ln2_b": row(ln2_b[layer]),
        }
        y_prompt = _encoder_layer(y_prompt, p, consts, alpha)
        y_sample = _encoder_layer(y_sample, p, consts, alpha)
    return (y_prompt, y_sample)
```

```python
import functools
import math

import numpy as np
import jax
import jax.numpy as jnp
from jax import lax
from jax.experimental import pallas as pl
from jax.experimental.pallas import tpu as pltpu

D_MODEL = 1024
HEAD_DIM = 64
A_GROUPS = ((128, 1), (512, 4), (2048, 16))
A_DILATIONS = tuple(d for _, d in A_GROUPS)
A_HEADS_PER_GROUP = 4
A_GROUP_WIDTH = A_HEADS_PER_GROUP * HEAD_DIM
A_WIDTH = len(A_GROUPS) * A_GROUP_WIDTH
A_SIDE = 64
A_QBLOCK = 128
A_KWIN = A_QBLOCK + 2 * A_SIDE
A_TILE = A_QBLOCK * max(A_DILATIONS)
A_PIPE_WIDTH = 2
B_HEADS = 8
B_WIDTH = B_HEADS * HEAD_DIM
GRID_W = 64
NA_ROWS = 8
NA_COLS = 16
NA_QROWS = 8
NA_QCOLS = 16
NA_KROWS = 16
NA_KCOLS = 32
NA_PIECE_ROWS = 4
ROPE_THETA = 500000.0
ROPE_DIMS = HEAD_DIM // 4
D_FF = 4 * D_MODEL
FF_CHUNK = 1024
QKV_COLS = 3 * A_WIDTH + 3 * B_WIDTH
LN_EPS = 1e-5
NEG_INF = -1e30
LOG2E = math.log2(math.e)
LANES = 128

VMEM_LIMIT = 56 * 1024 * 1024

_BF16 = jnp.bfloat16
_F32 = jnp.float32


def _dot(a, b):
    return jnp.dot(a, b, preferred_element_type=_F32)


def _dot_nt(a, b):
    return lax.dot_general(a, b, (((1,), (1,)), ((), ())), preferred_element_type=_F32)


def _resident(shape):
    nd = len(shape)
    return pl.BlockSpec(shape, lambda *_: (0,) * nd, pipeline_mode=pl.Buffered(1))


def _head_select(rows_per_head, width):
    n = A_HEADS_PER_GROUP * rows_per_head
    rh = lax.broadcasted_iota(jnp.int32, (n, width), 0) // rows_per_head
    lh = lax.broadcasted_iota(jnp.int32, (n, width), 1) // HEAD_DIM
    return rh == lh


def _diag_blocks(t, nq, left):
    cols = (lambda a, c: a) if t.shape[1] == 1 else (lambda a, c: a[:, c * LANES:(c + 1) * LANES])
    halves = [jnp.where(left, cols(t[(2 * c) * nq:(2 * c + 1) * nq], c), cols(t[(2 * c + 1) * nq:(2 * c + 2) * nq], c))
              for c in range(A_GROUP_WIDTH // LANES)]
    return jnp.concatenate(halves, axis=1)


def _proj_kernel(x_ref, w_ref, cos_ref, sa_ref, sb_ref,
                 q1_ref, k1_ref, v1_ref, q2_ref, k2_ref, v2_ref, q3_ref, k3_ref, v3_ref,
                 qb_ref, kb_ref, vb_ref, scr_ref):
    tm = x_ref.shape[0]
    xb = x_ref[...].astype(_BF16)
    cos = cos_ref[...]
    sa = sa_ref[...]
    sb = sb_ref[...]

    def rope(acc):
        outs = []
        for c in range(A_WIDTH // LANES):
            ch = acc[:, c * LANES:(c + 1) * LANES]
            up = pltpu.roll(ch, LANES - ROPE_DIMS // 2, 1)
            dn = pltpu.roll(ch, ROPE_DIMS // 2, 1)
            outs.append(ch * cos + up * sa + dn * sb)
        return jnp.concatenate(outs, axis=1)

    def emit(acc, outs, slot):
        gw = A_GROUP_WIDTH
        outs[0][...] = acc[:, 0:gw].astype(_BF16)
        for g in (1, 2):
            d = A_DILATIONS[g]
            for c in range(gw // LANES):
                buf = scr_ref.at[(2 * slot + g - 1) * (gw // LANES) + c]
                buf[...] = acc[:, g * gw + c * LANES:g * gw + (c + 1) * LANES]
                for r in range(d):
                    outs[g][r, :, c * LANES:(c + 1) * LANES] = buf[pl.ds(r, tm // d, stride=d), :].astype(_BF16)

    a = A_WIDTH
    c = B_WIDTH
    emit(rope(_dot(xb, w_ref[:, 0:a])), (q1_ref, q2_ref, q3_ref), 0)
    emit(rope(_dot(xb, w_ref[:, a:2 * a])), (k1_ref, k2_ref, k3_ref), 1)
    emit(_dot(xb, w_ref[:, 2 * a:3 * a]), (v1_ref, v2_ref, v3_ref), 2)
    qb_ref[...] = _dot(xb, w_ref[:, 3 * a:3 * a + c]).astype(_BF16)
    kb_ref[...] = _dot(xb, w_ref[:, 3 * a + c:3 * a + 2 * c]).astype(_BF16)
    vb_ref[...] = _dot(xb, w_ref[:, 3 * a + 2 * c:3 * a + 3 * c]).astype(_BF16)


def _rope_tables(seq):
    half = ROPE_DIMS // 2
    inv = ROPE_THETA ** (-jnp.arange(half, dtype=_F32) / half)
    ang = jnp.arange(seq, dtype=_F32)[:, None] * inv[None, :]
    cos = jnp.cos(ang)
    sin = jnp.sin(ang)
    ones = jnp.ones((seq, HEAD_DIM - ROPE_DIMS), _F32)
    zeros = jnp.zeros((seq, HEAD_DIM - ROPE_DIMS), _F32)
    zh = jnp.zeros((seq, half), _F32)
    cos_h = jnp.concatenate([cos, cos, ones], axis=1)
    sa_h = jnp.concatenate([-sin, zh, zeros], axis=1)
    sb_h = jnp.concatenate([zh, sin, zeros], axis=1)
    rep = LANES // HEAD_DIM
    return (jnp.tile(cos_h, (1, rep)), jnp.tile(sa_h, (1, rep)), jnp.tile(sb_h, (1, rep)))


def _project(x, w_qkv, tables, tm=512):
    bsz, seq, d = x.shape
    cos, sa, sb = tables
    grid = (bsz, seq // tm)
    tok = lambda w: pl.BlockSpec((None, tm, w), lambda b, t: (b, t, 0))
    tab = pl.BlockSpec((tm, LANES), lambda b, t: (t, 0))
    gw = A_GROUP_WIDTH
    a_specs, a_shapes = [], []
    for dil in A_DILATIONS:
        if dil == 1:
            a_specs.append(tok(gw))
            a_shapes.append(jax.ShapeDtypeStruct((bsz, seq, gw), _BF16))
        else:
            a_specs.append(pl.BlockSpec((None, dil, tm // dil, gw), lambda b, t: (b, 0, t, 0)))
            a_shapes.append(jax.ShapeDtypeStruct((bsz, dil, seq // dil, gw), _BF16))
    b_shape = jax.ShapeDtypeStruct((bsz, seq, B_WIDTH), _BF16)
    out_specs = [a_specs[g] for g in range(3) for _ in range(3)] + [tok(B_WIDTH)] * 3
    out_shape = [a_shapes[g] for g in range(3) for _ in range(3)] + [b_shape] * 3
    return pl.pallas_call(
        _proj_kernel,
        grid=grid,
        in_specs=[tok(d), _resident(w_qkv.shape), tab, tab, tab],
        out_specs=out_specs,
        out_shape=out_shape,
        scratch_shapes=[pltpu.VMEM((6 * gw // LANES, tm, LANES), _F32)],
        compiler_params=pltpu.CompilerParams(
            dimension_semantics=("parallel", "parallel"), vmem_limit_bytes=VMEM_LIMIT),
        name="proj_rope",
    )(x, w_qkv, cos, sa, sb)


def _mixer_a_kernel(mask_ref, q1_ref, q2_ref, q3_ref, k1_ref, k2_ref, k3_ref, v1_ref, v2_ref, v3_ref,
                    y_ref, o2_scr, l2_scr, o3_scr, l3_scr, s_scr, p_scr, inv_scr, l1_scr, *, seq):
    t = pl.program_id(1)
    qb = A_QBLOCK
    nh = A_HEADS_PER_GROUP
    nblk = A_TILE // qb
    sel = _head_select(qb, A_GROUP_WIDTH)
    left = lax.broadcasted_iota(jnp.int32, (qb, LANES), 1) < HEAD_DIM
    nhalf = A_GROUP_WIDTH // LANES
    q_refs = (q1_ref, q2_ref, q3_ref)
    k_refs = (k1_ref, k2_ref, k3_ref)
    v_refs = (v1_ref, v2_ref, v3_ref)

    def place(g, n):
        d = A_DILATIONS[g]
        per_res = nblk // d
        r, i = n // per_res, n % per_res
        sub_len = seq // d
        l0 = t * (A_TILE // d) + i * qb
        ks = pl.multiple_of(jnp.clip(l0 - A_SIDE, 0, sub_len - A_KWIN), A_SIDE)
        case = jnp.where(l0 == 0, 0, jnp.where(l0 == sub_len - qb, 2, 1))
        row0 = i * qb
        if not isinstance(row0, int):
            row0 = pl.multiple_of(row0, qb)
        return r, row0, ks, case

    def plane(refs, g, r):
        return refs[g] if A_DILATIONS[g] == 1 else refs[g].at[r]

    def scores(g, n, slot):
        r, row0, ks, _ = place(g, n)
        q = plane(q_refs, g, r)[pl.ds(row0, qb), :]
        qexp = jnp.where(sel, jnp.concatenate([q] * nh, axis=0), jnp.zeros((), _BF16))
        s_scr[slot] = _dot_nt(qexp, plane(k_refs, g, r)[pl.ds(ks, A_KWIN), :])

    def softmax(g, n, slot):
        r, row0, _, case = place(g, n)
        mask = mask_ref[case]
        maxes, dens = [], []
        s_blk = s_scr.at[slot]
        p_blk = p_scr.at[slot]
        for h in range(nh):
            rows = slice(h * qb, (h + 1) * qb)
            s = s_blk[rows, :] + mask
            m = jnp.max(s, axis=-1, keepdims=True)
            e = jnp.exp2(s - m)
            p_blk[rows, :] = e.astype(_BF16)
            maxes.append(m)
            dens.append(jnp.sum(e, axis=-1, keepdims=True))
        pair = lambda v, c: jnp.where(left, v[2 * c], v[2 * c + 1])
        den = [pair(dens, c) for c in range(nhalf)]
        inv_scr[slot] = jnp.concatenate([1.0 / dc for dc in den], axis=1)
        lse = jnp.concatenate([pair(maxes, c) + jnp.log2(den[c]) for c in range(nhalf)], axis=1)
        if g == 0:
            l1_scr[slot] = lse
        else:
            d = A_DILATIONS[g]
            scatter((l2_scr, l3_scr)[g - 1], row0 * d + r, d, lse)

    def output(g, n, slot):
        r, row0, ks, _ = place(g, n)
        vw = plane(v_refs, g, r)[pl.ds(ks, A_KWIN), :]
        p_blk = p_scr.at[slot]
        halves = []
        for c in range(nhalf):
            pv = _dot(p_blk[2 * c * qb:2 * (c + 1) * qb, :], vw[:, c * LANES:(c + 1) * LANES])
            halves.append(jnp.where(left, pv[:qb], pv[qb:]))
        o = jnp.concatenate(halves, axis=1) * inv_scr[slot]
        if g != 0:
            d = A_DILATIONS[g]
            scatter((o2_scr, o3_scr)[g - 1], row0 * d + r, d, o)
            return
        rows = pl.ds(row0, qb)
        l1 = l1_scr[slot]
        l2 = gather(l2_scr, rows)
        l3 = gather(l3_scr, rows)
        lm = jnp.maximum(jnp.maximum(l1, l2), l3)
        e1 = jnp.exp2(l1 - lm)
        e2 = jnp.exp2(l2 - lm)
        e3 = jnp.exp2(l3 - lm)
        ya = (e1 * o + e2 * gather(o2_scr, rows) + e3 * gather(o3_scr, rows)) * (1.0 / (e1 + e2 + e3))
        y_ref[rows, :] = ya.astype(_BF16)

    def scatter(scr, start, stride, val):
        for c in range(nhalf):
            scr[c, pl.ds(start, qb, stride=stride), :] = val[:, c * LANES:(c + 1) * LANES]

    def gather(scr, rows):
        return jnp.concatenate([scr[c, rows, :] for c in range(nhalf)], axis=1)

    order = (2, 1, 0)
    wide = A_PIPE_WIDTH
    steps_per_group = nblk // wide
    total = steps_per_group * len(order)

    def step(k, g_of, local_of, parity):
        for w in range(wide):
            if 0 <= k < total:
                output(g_of(k), wide * local_of(k) + w, (parity, w))
        for w in range(wide):
            if k + 2 < total:
                scores(g_of(k + 2), wide * local_of(k + 2) + w, (parity, w))
        for w in range(wide):
            if 0 <= k + 1 < total:
                softmax(g_of(k + 1), wide * local_of(k + 1) + w, (1 - parity, w))

    static_g = lambda j: order[j // steps_per_group]
    static_local = lambda j: j % steps_per_group
    step(-2, static_g, static_local, 0)
    step(-1, static_g, static_local, 1)
    for gi, g in enumerate(order):
        base = gi * steps_per_group

        def steady(m, carry, g=g, base=base):
            for par in (0, 1):
                step(base + par, lambda j: g, lambda j: 2 * m + (j - base), par)
            return carry

        lax.fori_loop(0, (steps_per_group - 2) // 2, steady, 0)
        for k in (base + steps_per_group - 2, base + steps_per_group - 1):
            step(k, static_g, static_local, k % 2)


def _a_masks():
    qi = np.arange(A_QBLOCK)[:, None]
    kj = np.arange(A_KWIN)[None, :]
    out = []
    for off in (0, -A_SIDE, -2 * A_SIDE):
        ok = np.abs(kj + off - qi) <= A_SIDE
        out.append(np.where(ok, 0.0, NEG_INF))
    return jnp.asarray(np.stack(out), _F32)


def _mixer_a(qkv, masks):
    q1, k1, v1, q2, k2, v2, q3, k3, v3 = qkv
    bsz, seq, gw = q1.shape
    assert seq % A_TILE == 0 and seq // max(A_DILATIONS) >= A_KWIN

    def q_spec(dil):
        if dil == 1:
            return pl.BlockSpec((None, A_TILE, gw), lambda b, t: (b, t, 0))
        return pl.BlockSpec((None, dil, A_TILE // dil, gw), lambda b, t: (b, 0, t, 0))

    def kv_spec(dil):
        if dil == 1:
            return pl.BlockSpec((None, seq, gw), lambda b, t: (b, 0, 0), pipeline_mode=pl.Buffered(1))
        return pl.BlockSpec((None, dil, seq // dil, gw), lambda b, t: (b, 0, 0, 0), pipeline_mode=pl.Buffered(1))

    qs = [q_spec(d) for d in A_DILATIONS]
    kvs = [kv_spec(d) for d in A_DILATIONS]
    return pl.pallas_call(
        functools.partial(_mixer_a_kernel, seq=seq),
        grid=(bsz, seq // A_TILE),
        in_specs=[_resident(masks.shape)] + qs + kvs + kvs,
        out_specs=pl.BlockSpec((None, A_TILE, gw), lambda b, t: (b, t, 0)),
        out_shape=jax.ShapeDtypeStruct((bsz, seq, gw), _BF16),
        scratch_shapes=[pltpu.VMEM((gw // LANES, A_TILE, LANES), _F32)] * 4 + [
            pltpu.VMEM((2, A_PIPE_WIDTH, A_HEADS_PER_GROUP * A_QBLOCK, A_KWIN), _F32),
            pltpu.VMEM((2, A_PIPE_WIDTH, A_HEADS_PER_GROUP * A_QBLOCK, A_KWIN), _BF16),
            pltpu.VMEM((2, A_PIPE_WIDTH, A_QBLOCK, gw), _F32),
            pltpu.VMEM((2, A_PIPE_WIDTH, A_QBLOCK, gw), _F32),
        ],
        compiler_params=pltpu.CompilerParams(
            dimension_semantics=("parallel", "arbitrary"), vmem_limit_bytes=VMEM_LIMIT),
        name="mixer_a",
    )(masks, q1, q2, q3, k1, k2, k3, v1, v2, v3)


def _na_col_start(j):
    return min(max(j * NA_QCOLS - NA_COLS // 2, 0), GRID_W - NA_KCOLS)


def _na_bias_table(rpb, rows):
    nrb = rows // NA_QROWS
    qr = np.arange(NA_QROWS)
    qc = np.arange(NA_QCOLS)
    kr = np.arange(NA_KROWS)
    kc = np.arange(NA_KCOLS)
    dr_all, okr_all = [], []
    for i in (0, 1, nrb - 1):
        r_abs = i * NA_QROWS + qr
        win_r = np.clip(r_abs - NA_ROWS // 2, 0, rows - NA_ROWS)
        k_abs = min(max(i * NA_QROWS - NA_ROWS // 2, 0), rows - NA_KROWS) + kr
        okr_all.append((k_abs[None, :] >= win_r[:, None]) & (k_abs[None, :] < win_r[:, None] + NA_ROWS))
        dr_all.append(np.clip(k_abs[None, :] - r_abs[:, None] + NA_ROWS - 1, 0, 2 * NA_ROWS - 2))
    dc_all, okc_all = [], []
    for j in range(GRID_W // NA_QCOLS):
        c_abs = j * NA_QCOLS + qc
        win_c = np.clip(c_abs - NA_COLS // 2, 0, GRID_W - NA_COLS)
        k_abs = _na_col_start(j) + kc
        okc_all.append((k_abs[None, :] >= win_c[:, None]) & (k_abs[None, :] < win_c[:, None] + NA_COLS))
        dc_all.append(np.clip(k_abs[None, :] - c_abs[:, None] + NA_COLS - 1, 0, 2 * NA_COLS - 2))
    dr = np.stack(dr_all)
    okr = np.stack(okr_all)
    dc = np.stack(dc_all)
    okc = np.stack(okc_all)
    rsel = jnp.asarray(np.eye(2 * NA_ROWS - 1, dtype=np.float32)[dr])
    csel = jnp.asarray(np.eye(2 * NA_COLS - 1, dtype=np.float32)[dc])
    hi = lax.Precision.HIGHEST
    t1 = jnp.einsum("hab,cqkb->hacqk", rpb.astype(_F32), csel, precision=hi)
    bias = jnp.einsum("rpsa,hacqk->rchpqsk", rsel, t1, precision=hi)
    ok = okr[:, None, None, :, None, :, None] & okc[None, :, None, None, :, None, :]
    bias = jnp.where(jnp.asarray(ok), bias * LOG2E, NEG_INF)
    return bias.reshape(3, GRID_W // NA_QCOLS, B_HEADS, NA_QROWS * NA_QCOLS, NA_KROWS * NA_KCOLS)


def _attn_b_kernel(bias_ref, q_ref, k0_ref, k1_ref, k2_ref, k3_ref,
                   v0_ref, v1_ref, v2_ref, v3_ref, o_ref, s_scr, p_scr, inv_scr):
    k_refs = (k0_ref, k1_ref, k2_ref, k3_ref)
    v_refs = (v0_ref, v1_ref, v2_ref, v3_ref)
    nq = NA_QROWS * NA_QCOLS
    nh = A_HEADS_PER_GROUP
    half_w = nh * HEAD_DIM
    nhalves = B_HEADS // nh
    sel = _head_select(nq, half_w)
    left = lax.broadcasted_iota(jnp.int32, (nq, LANES), 1) < HEAD_DIM

    def lanes_of(b):
        hh = b % nhalves
        return slice(hh * half_w, (hh + 1) * half_w)

    def window(refs, b):
        cs = _na_col_start(b // nhalves)
        return jnp.concatenate(
            [refs[kr // NA_PIECE_ROWS][(kr % NA_PIECE_ROWS) * GRID_W + cs:
                                       (kr % NA_PIECE_ROWS) * GRID_W + cs + NA_KCOLS, lanes_of(b)]
             for kr in range(NA_KROWS)], axis=0)

    def scores(b, slot):
        j = b // nhalves
        q = jnp.concatenate(
            [q_ref[r * GRID_W + j * NA_QCOLS:r * GRID_W + (j + 1) * NA_QCOLS, lanes_of(b)]
             for r in range(NA_QROWS)], axis=0)
        qexp = jnp.where(sel, jnp.concatenate([q] * nh, axis=0), jnp.zeros((), _BF16))
        s_scr[slot] = _dot_nt(qexp, window(k_refs, b))

    def softmax(b, slot):
        j, hh = b // nhalves, b % nhalves
        dens = []
        for h in range(nh):
            rows = slice(h * nq, (h + 1) * nq)
            s = s_scr[slot, rows, :] + bias_ref[j, hh * nh + h]
            e = jnp.exp2(s - jnp.max(s, axis=-1, keepdims=True))
            dens.append(jnp.sum(e, axis=-1, keepdims=True))
            p_scr[slot, rows, :] = e.astype(_BF16)
        inv_scr[slot] = jnp.concatenate(
            [1.0 / jnp.where(left, dens[2 * c], dens[2 * c + 1]) for c in range(half_w // LANES)], axis=1)

    def output(b, slot):
        j = b // nhalves
        pv = _dot(p_scr[slot], window(v_refs, b))
        ob = (_diag_blocks(pv, nq, left) * inv_scr[slot]).astype(_BF16)
        for r in range(NA_QROWS):
            o_ref[r * GRID_W + j * NA_QCOLS:r * GRID_W + (j + 1) * NA_QCOLS, lanes_of(b)] = (
                ob[r * NA_QCOLS:(r + 1) * NA_QCOLS, :])

    total = (GRID_W // NA_QCOLS) * nhalves
    for it in range(total + 2):
        if it < total:
            scores(it, it % 2)
        if 1 <= it <= total:
            softmax(it - 1, (it - 1) % 2)
        if it >= 2:
            output(it - 2, it % 2)


def _mixer_b(qb, kb, vb, bias_table):
    bsz, seq, _ = qb.shape
    rows = seq // GRID_W
    nrb = rows // NA_QROWS
    q_tokens = NA_QROWS * GRID_W
    piece_tokens = NA_PIECE_ROWS * GRID_W
    npieces = NA_KROWS // NA_PIECE_ROWS
    last_piece = rows // NA_PIECE_ROWS - npieces

    def piece0(i):
        return jnp.clip(i * (NA_QROWS // NA_PIECE_ROWS) - 1, 0, last_piece)

    def row_case(i):
        return jnp.where(i == 0, 0, jnp.where(i == nrb - 1, 2, 1))

    bias_spec = pl.BlockSpec((None,) + bias_table.shape[1:], lambda b, i: (row_case(i), 0, 0, 0, 0))
    q_spec = pl.BlockSpec((None, q_tokens, B_WIDTH), lambda b, i: (b, i, 0))
    piece_specs = [pl.BlockSpec((None, piece_tokens, B_WIDTH), lambda b, i, n=n: (b, piece0(i) + n, 0))
                   for n in range(npieces)]
    return pl.pallas_call(
        _attn_b_kernel,
        grid=(bsz, nrb),
        in_specs=[bias_spec, q_spec] + piece_specs + piece_specs,
        out_specs=q_spec,
        out_shape=jax.ShapeDtypeStruct((bsz, seq, B_WIDTH), _BF16),
        scratch_shapes=[
            pltpu.VMEM((2, A_HEADS_PER_GROUP * NA_QROWS * NA_QCOLS, NA_KROWS * NA_KCOLS), _F32),
            pltpu.VMEM((2, A_HEADS_PER_GROUP * NA_QROWS * NA_QCOLS, NA_KROWS * NA_KCOLS), _BF16),
            pltpu.VMEM((2, NA_QROWS * NA_QCOLS, A_GROUP_WIDTH), _F32),
        ],
        compiler_params=pltpu.CompilerParams(
            dimension_semantics=("parallel", "arbitrary"), vmem_limit_bytes=VMEM_LIMIT),
        name="mixer_b",
    )(bias_table, qb, kb, kb, kb, kb, vb, vb, vb, vb)


def _layer_norm(h, g, b):
    mu = jnp.mean(h, axis=-1, keepdims=True)
    c = h - mu
    var = jnp.mean(c * c, axis=-1, keepdims=True)
    return c * lax.rsqrt(var + LN_EPS) * g + b


def _post_kernel(x_ref, ya_ref, yb_ref,
                 wg_ref, bg_ref, wa_ref, wb_ref, wo_ref, ln1g_ref, ln1b_ref,
                 w1_ref, b1_ref, w2_ref, b2_ref, ln2g_ref, ln2b_ref, y_ref, *, alpha):
    x = x_ref[...]
    xb = x.astype(_BF16)
    ya = _dot(ya_ref[...], wa_ref[...])
    yb = _dot(yb_ref[...], wb_ref[...])
    ga = jax.nn.sigmoid(_dot(xb, wg_ref[:, :D_MODEL]) + bg_ref[0:1, :])
    gb = jax.nn.sigmoid(_dot(xb, wg_ref[:, D_MODEL:]) + bg_ref[1:2, :])
    merged = ga * ya + gb * yb
    h = alpha * x + _dot(merged.astype(_BF16), wo_ref[...])
    x1 = _layer_norm(h, ln1g_ref[...], ln1b_ref[...])
    x1b = x1.astype(_BF16)
    h2 = alpha * x1 + b2_ref[...]
    for c in range(D_FF // FF_CHUNK):
        cols = slice(c * FF_CHUNK, (c + 1) * FF_CHUNK)
        hid = jnp.maximum(_dot(x1b, w1_ref[:, cols]) + b1_ref[:, cols], 0.0)
        h2 = h2 + _dot((hid * hid).astype(_BF16), w2_ref[cols, :])
    y_ref[...] = _layer_norm(h2, ln2g_ref[...], ln2b_ref[...])


def _post(x, ya, yb, p, alpha, tm=512):
    bsz, seq, d = x.shape
    tok = lambda w: pl.BlockSpec((None, tm, w), lambda b, t: (b, t, 0))
    weights = (p["w_gate"], p["b_gate"], p["w_a"], p["w_b"], p["w_o"], p["ln1_g"], p["ln1_b"],
               p["w_ff1"], p["b_ff1"], p["w_ff2"], p["b_ff2"], p["ln2_g"], p["ln2_b"])
    return pl.pallas_call(
        functools.partial(_post_kernel, alpha=alpha),
        grid=(bsz, seq // tm),
        in_specs=[tok(d), tok(A_GROUP_WIDTH), tok(B_WIDTH)] + [_resident(w.shape) for w in weights],
        out_specs=tok(d),
        out_shape=jax.ShapeDtypeStruct((bsz, seq, d), _F32),
        compiler_params=pltpu.CompilerParams(
            dimension_semantics=("parallel", "parallel"), vmem_limit_bytes=VMEM_LIMIT),
        name="post_ffn",
    )(x, ya, yb, *weights)


def _encoder_layer(x, p, consts, alpha):
    outs = _project(x, p["w_qkv"], consts["rope"])
    ya = _mixer_a(outs[:9], consts["a_masks"])
    yb = _mixer_b(*outs[9:], p["na_bias"])
    return _post(x, ya, yb, p, alpha)


def kernel(x_prompt, x_sample, w_in, b_gate, w_branch_a, w_branch_b, w_out, rel_pos_bias,
           ln1_g, ln1_b, w_ff1, b_ff1, w_ff2, b_ff2, ln2_g, ln2_b):
    depth = w_in.shape[0]
    alpha = (2.0 * depth) ** 0.25
    seq = x_prompt.shape[1]
    assert x_sample.shape[1] == seq and seq % (GRID_W * NA_KROWS) == 0
    consts = {"rope": _rope_tables(seq), "a_masks": _a_masks()}
    scale = HEAD_DIM ** -0.5 * LOG2E
    col_scale = np.ones((QKV_COLS,), np.float32)
    col_scale[0:A_WIDTH] = scale
    col_scale[3 * A_WIDTH:3 * A_WIDTH + B_WIDTH] = scale
    row = lambda v: v.reshape(1, -1)
    y_prompt, y_sample = x_prompt, x_sample
    for layer in range(depth):
        p = {
            "w_qkv": (w_in[layer][:, :QKV_COLS] * col_scale).astype(_BF16),
            "w_gate": w_in[layer][:, QKV_COLS:].astype(_BF16),
            "b_gate": b_gate[layer],
            "w_a": w_branch_a[layer].astype(_BF16),
            "w_b": w_branch_b[layer].astype(_BF16),
            "w_o": w_out[layer].astype(_BF16),
            "na_bias": _na_bias_table(rel_pos_bias[layer], seq // GRID_W),
            "ln1_g": row(ln1_g[layer]), "ln1_b": row(ln1_b[layer]),
            "w_ff1": w_ff1[layer].astype(_BF16), "b_ff1": row(b_ff1[layer]),
            "w_ff2": w_ff2[layer].astype(_BF16), "b_ff2": row(b_ff2[layer]),
            "ln2_g": row(ln2_g[layer]), "ln2_b": row(ln2_b[layer]),
        }
        y_prompt = _encoder_layer(y_prompt, p, consts, alpha)
        y_sample = _encoder_layer(y_sample, p, consts, alpha)
    return (y_prompt, y_sample)
```

```python
import functools
import math

import numpy as np
import jax
import jax.numpy as jnp
from jax import lax
from jax.experimental import pallas as pl
from jax.experimental.pallas import tpu as pltpu

D_MODEL = 1024
HEAD_DIM = 64
A_GROUPS = ((128, 1), (512, 4), (2048, 16))
A_DILATIONS = tuple(d for _, d in A_GROUPS)
A_HEADS_PER_GROUP = 4
A_GROUP_WIDTH = A_HEADS_PER_GROUP * HEAD_DIM
A_WIDTH = len(A_GROUPS) * A_GROUP_WIDTH
A_SIDE = 64
A_QBLOCK = 128
A_KWIN = A_QBLOCK + 2 * A_SIDE
A_TILE = A_QBLOCK * max(A_DILATIONS)
A_PIPE_WIDTH = 2
B_HEADS = 8
B_WIDTH = B_HEADS * HEAD_DIM
GRID_W = 64
NA_ROWS = 8
NA_COLS = 16
NA_QROWS = 8
NA_QCOLS = 16
NA_KROWS = 16
NA_KCOLS = 32
NA_PIECE_ROWS = 4
ROPE_THETA = 500000.0
ROPE_DIMS = HEAD_DIM // 4
D_FF = 4 * D_MODEL
FF_CHUNK = 1024
QKV_COLS = 3 * A_WIDTH + 3 * B_WIDTH
LN_EPS = 1e-5
NEG_INF = -1e30
LOG2E = math.log2(math.e)
LANES = 128

VMEM_LIMIT = 56 * 1024 * 1024

_BF16 = jnp.bfloat16
_F32 = jnp.float32


def _dot(a, b):
    return jnp.dot(a, b, preferred_element_type=_F32)


def _dot_nt(a, b):
    return lax.dot_general(a, b, (((1,), (1,)), ((), ())), preferred_element_type=_F32)


def _resident(shape):
    nd = len(shape)
    return pl.BlockSpec(shape, lambda *_: (0,) * nd, pipeline_mode=pl.Buffered(1))


def _head_select(rows_per_head, width):
    n = A_HEADS_PER_GROUP * rows_per_head
    rh = lax.broadcasted_iota(jnp.int32, (n, width), 0) // rows_per_head
    lh = lax.broadcasted_iota(jnp.int32, (n, width), 1) // HEAD_DIM
    return rh == lh


def _diag_blocks(t, nq, left):
    cols = (lambda a, c: a) if t.shape[1] == 1 else (lambda a, c: a[:, c * LANES:(c + 1) * LANES])
    halves = [jnp.where(left, cols(t[(2 * c) * nq:(2 * c + 1) * nq], c), cols(t[(2 * c + 1) * nq:(2 * c + 2) * nq], c))
              for c in range(A_GROUP_WIDTH // LANES)]
    return jnp.concatenate(halves, axis=1)


def _proj_kernel(x_ref, w_ref, cos_ref, sa_ref, sb_ref,
                 q1_ref, k1_ref, v1_ref, q2_ref, k2_ref, v2_ref, q3_ref, k3_ref, v3_ref,
                 qb_ref, kb_ref, vb_ref, scr_ref):
    tm = x_ref.shape[0]
    xb = x_ref[...].astype(_BF16)
    cos = cos_ref[...]
    sa = sa_ref[...]
    sb = sb_ref[...]

    def rope(acc):
        outs = []
        for c in range(A_WIDTH // LANES):
            ch = acc[:, c * LANES:(c + 1) * LANES]
            up = pltpu.roll(ch, LANES - ROPE_DIMS // 2, 1)
            dn = pltpu.roll(ch, ROPE_DIMS // 2, 1)
            outs.append(ch * cos + up * sa + dn * sb)
        return jnp.concatenate(outs, axis=1)

    def emit(acc, outs, slot):
        gw = A_GROUP_WIDTH
        outs[0][...] = acc[:, 0:gw].astype(_BF16)
        for g in (1, 2):
            d = A_DILATIONS[g]
            for c in range(gw // LANES):
                buf = scr_ref.at[(2 * slot + g - 1) * (gw // LANES) + c]
                buf[...] = acc[:, g * gw + c * LANES:g * gw + (c + 1) * LANES]
                for r in range(d):
                    outs[g][r, :, c * LANES:(c + 1) * LANES] = buf[pl.ds(r, tm // d, stride=d), :].astype(_BF16)

    a = A_WIDTH
    c = B_WIDTH
    emit(rope(_dot(xb, w_ref[:, 0:a])), (q1_ref, q2_ref, q3_ref), 0)
    emit(rope(_dot(xb, w_ref[:, a:2 * a])), (k1_ref, k2_ref, k3_ref), 1)
    emit(_dot(xb, w_ref[:, 2 * a:3 * a]), (v1_ref, v2_ref, v3_ref), 2)
    qb_ref[...] = _dot(xb, w_ref[:, 3 * a:3 * a + c]).astype(_BF16)
    kb_ref[...] = _dot(xb, w_ref[:, 3 * a + c:3 * a + 2 * c]).astype(_BF16)
    vb_ref[...] = _dot(xb, w_ref[:, 3 * a + 2 * c:3 * a + 3 * c]).astype(_BF16)


def _rope_tables(seq):
    half = ROPE_DIMS // 2
    inv = ROPE_THETA ** (-jnp.arange(half, dtype=_F32) / half)
    ang = jnp.arange(seq, dtype=_F32)[:, None] * inv[None, :]
    cos = jnp.cos(ang)
    sin = jnp.sin(ang)
    ones = jnp.ones((seq, HEAD_DIM - ROPE_DIMS), _F32)
    zeros = jnp.zeros((seq, HEAD_DIM - ROPE_DIMS), _F32)
    zh = jnp.zeros((seq, half), _F32)
    cos_h = jnp.concatenate([cos, cos, ones], axis=1)
    sa_h = jnp.concatenate([-sin, zh, zeros], axis=1)
    sb_h = jnp.concatenate([zh, sin, zeros], axis=1)
    rep = LANES // HEAD_DIM
    return (jnp.tile(cos_h, (1, rep)), jnp.tile(sa_h, (1, rep)), jnp.tile(sb_h, (1, rep)))


def _project(x, w_qkv, tables, tm=1024):
    bsz, seq, d = x.shape
    cos, sa, sb = tables
    grid = (bsz, seq // tm)
    tok = lambda w: pl.BlockSpec((None, tm, w), lambda b, t: (b, t, 0))
    tab = pl.BlockSpec((tm, LANES), lambda b, t: (t, 0))
    gw = A_GROUP_WIDTH
    a_specs, a_shapes = [], []
    for dil in A_DILATIONS:
        if dil == 1:
            a_specs.append(tok(gw))
            a_shapes.append(jax.ShapeDtypeStruct((bsz, seq, gw), _BF16))
        else:
            a_specs.append(pl.BlockSpec((None, dil, tm // dil, gw), lambda b, t: (b, 0, t, 0)))
            a_shapes.append(jax.ShapeDtypeStruct((bsz, dil, seq // dil, gw), _BF16))
    b_shape = jax.ShapeDtypeStruct((bsz, seq, B_WIDTH), _BF16)
    out_specs = [a_specs[g] for g in range(3) for _ in range(3)] + [tok(B_WIDTH)] * 3
    out_shape = [a_shapes[g] for g in range(3) for _ in range(3)] + [b_shape] * 3
    return pl.pallas_call(
        _proj_kernel,
        grid=grid,
        in_specs=[tok(d), _resident(w_qkv.shape), tab, tab, tab],
        out_specs=out_specs,
        out_shape=out_shape,
        scratch_shapes=[pltpu.VMEM((6 * gw // LANES, tm, LANES), _F32)],
        compiler_params=pltpu.CompilerParams(
            dimension_semantics=("parallel", "parallel"), vmem_limit_bytes=VMEM_LIMIT),
        name="proj_rope",
    )(x, w_qkv, cos, sa, sb)


def _mixer_a_kernel(mask_ref, q1_ref, q2_ref, q3_ref, k1_ref, k2_ref, k3_ref, v1_ref, v2_ref, v3_ref,
                    y_ref, o2_scr, l2_scr, o3_scr, l3_scr, s_scr, p_scr, inv_scr, l1_scr, *, seq):
    t = pl.program_id(1)
    qb = A_QBLOCK
    nh = A_HEADS_PER_GROUP
    nblk = A_TILE // qb
    sel = _head_select(qb, A_GROUP_WIDTH)
    left = lax.broadcasted_iota(jnp.int32, (qb, LANES), 1) < HEAD_DIM
    nhalf = A_GROUP_WIDTH // LANES
    q_refs = (q1_ref, q2_ref, q3_ref)
    k_refs = (k1_ref, k2_ref, k3_ref)
    v_refs = (v1_ref, v2_ref, v3_ref)

    def place(g, n):
        d = A_DILATIONS[g]
        per_res = nblk // d
        r, i = n // per_res, n % per_res
        sub_len = seq // d
        l0 = t * (A_TILE // d) + i * qb
        ks = pl.multiple_of(jnp.clip(l0 - A_SIDE, 0, sub_len - A_KWIN), A_SIDE)
        case = jnp.where(l0 == 0, 0, jnp.where(l0 == sub_len - qb, 2, 1))
        row0 = i * qb
        if not isinstance(row0, int):
            row0 = pl.multiple_of(row0, qb)
        return r, row0, ks, case

    def plane(refs, g, r):
        return refs[g] if A_DILATIONS[g] == 1 else refs[g].at[r]

    def scores(g, n, slot):
        r, row0, ks, _ = place(g, n)
        q = plane(q_refs, g, r)[pl.ds(row0, qb), :]
        qexp = jnp.where(sel, jnp.concatenate([q] * nh, axis=0), jnp.zeros((), _BF16))
        s_scr[slot] = _dot_nt(qexp, plane(k_refs, g, r)[pl.ds(ks, A_KWIN), :])

    def softmax(g, n, slot):
        r, row0, _, case = place(g, n)
        mask = mask_ref[case]
        maxes, dens = [], []
        s_blk = s_scr.at[slot]
        p_blk = p_scr.at[slot]
        for h in range(nh):
            rows = slice(h * qb, (h + 1) * qb)
            s = s_blk[rows, :] + mask
            m = jnp.max(s, axis=-1, keepdims=True)
            e = jnp.exp2(s - m)
            p_blk[rows, :] = e.astype(_BF16)
            maxes.append(m)
            dens.append(jnp.sum(e, axis=-1, keepdims=True))
        pair = lambda v, c: jnp.where(left, v[2 * c], v[2 * c + 1])
        den = [pair(dens, c) for c in range(nhalf)]
        inv_scr[slot] = jnp.concatenate([1.0 / dc for dc in den], axis=1)
        lse = jnp.concatenate([pair(maxes, c) + jnp.log2(den[c]) for c in range(nhalf)], axis=1)
        if g == 0:
            l1_scr[slot] = lse
        else:
            d = A_DILATIONS[g]
            scatter((l2_scr, l3_scr)[g - 1], row0 * d + r, d, lse)

    def output(g, n, slot):
        r, row0, ks, _ = place(g, n)
        vw = plane(v_refs, g, r)[pl.ds(ks, A_KWIN), :]
        p_blk = p_scr.at[slot]
        halves = []
        for c in range(nhalf):
            pv = _dot(p_blk[2 * c * qb:2 * (c + 1) * qb, :], vw[:, c * LANES:(c + 1) * LANES])
            halves.append(jnp.where(left, pv[:qb], pv[qb:]))
        o = jnp.concatenate(halves, axis=1) * inv_scr[slot]
        if g != 0:
            d = A_DILATIONS[g]
            scatter((o2_scr, o3_scr)[g - 1], row0 * d + r, d, o)
            return
        rows = pl.ds(row0, qb)
        l1 = l1_scr[slot]
        l2 = gather(l2_scr, rows)
        l3 = gather(l3_scr, rows)
        lm = jnp.maximum(jnp.maximum(l1, l2), l3)
        e1 = jnp.exp2(l1 - lm)
        e2 = jnp.exp2(l2 - lm)
        e3 = jnp.exp2(l3 - lm)
        ya = (e1 * o + e2 * gather(o2_scr, rows) + e3 * gather(o3_scr, rows)) * (1.0 / (e1 + e2 + e3))
        y_ref[rows, :] = ya.astype(_BF16)

    def scatter(scr, start, stride, val):
        for c in range(nhalf):
            scr[c, pl.ds(start, qb, stride=stride), :] = val[:, c * LANES:(c + 1) * LANES]

    def gather(scr, rows):
        return jnp.concatenate([scr[c, rows, :] for c in range(nhalf)], axis=1)

    order = (2, 1, 0)
    wide = A_PIPE_WIDTH
    steps_per_group = nblk // wide
    total = steps_per_group * len(order)

    def step(k, g_of, local_of, parity):
        for w in range(wide):
            if 0 <= k < total:
                output(g_of(k), wide * local_of(k) + w, (parity, w))
        for w in range(wide):
            if k + 2 < total:
                scores(g_of(k + 2), wide * local_of(k + 2) + w, (parity, w))
        for w in range(wide):
            if 0 <= k + 1 < total:
                softmax(g_of(k + 1), wide * local_of(k + 1) + w, (1 - parity, w))

    static_g = lambda j: order[j // steps_per_group]
    static_local = lambda j: j % steps_per_group
    step(-2, static_g, static_local, 0)
    step(-1, static_g, static_local, 1)
    for gi, g in enumerate(order):
        base = gi * steps_per_group

        def steady(m, carry, g=g, base=base):
            for par in (0, 1):
                step(base + par, lambda j: g, lambda j: 2 * m + (j - base), par)
            return carry

        lax.fori_loop(0, (steps_per_group - 2) // 2, steady, 0)
        for k in (base + steps_per_group - 2, base + steps_per_group - 1):
            step(k, static_g, static_local, k % 2)


def _a_masks():
    qi = np.arange(A_QBLOCK)[:, None]
    kj = np.arange(A_KWIN)[None, :]
    out = []
    for off in (0, -A_SIDE, -2 * A_SIDE):
        ok = np.abs(kj + off - qi) <= A_SIDE
        out.append(np.where(ok, 0.0, NEG_INF))
    return jnp.asarray(np.stack(out), _F32)


def _mixer_a(qkv, masks):
    q1, k1, v1, q2, k2, v2, q3, k3, v3 = qkv
    bsz, seq, gw = q1.shape
    assert seq % A_TILE == 0 and seq // max(A_DILATIONS) >= A_KWIN

    def q_spec(dil):
        if dil == 1:
            return pl.BlockSpec((None, A_TILE, gw), lambda b, t: (b, t, 0))
        return pl.BlockSpec((None, dil, A_TILE // dil, gw), lambda b, t: (b, 0, t, 0))

    def kv_spec(dil):
        if dil == 1:
            return pl.BlockSpec((None, seq, gw), lambda b, t: (b, 0, 0), pipeline_mode=pl.Buffered(1))
        return pl.BlockSpec((None, dil, seq // dil, gw), lambda b, t: (b, 0, 0, 0), pipeline_mode=pl.Buffered(1))

    qs = [q_spec(d) for d in A_DILATIONS]
    kvs = [kv_spec(d) for d in A_DILATIONS]
    return pl.pallas_call(
        functools.partial(_mixer_a_kernel, seq=seq),
        grid=(bsz, seq // A_TILE),
        in_specs=[_resident(masks.shape)] + qs + kvs + kvs,
        out_specs=pl.BlockSpec((None, A_TILE, gw), lambda b, t: (b, t, 0)),
        out_shape=jax.ShapeDtypeStruct((bsz, seq, gw), _BF16),
        scratch_shapes=[pltpu.VMEM((gw // LANES, A_TILE, LANES), _F32)] * 4 + [
            pltpu.VMEM((2, A_PIPE_WIDTH, A_HEADS_PER_GROUP * A_QBLOCK, A_KWIN), _F32),
            pltpu.VMEM((2, A_PIPE_WIDTH, A_HEADS_PER_GROUP * A_QBLOCK, A_KWIN), _BF16),
            pltpu.VMEM((2, A_PIPE_WIDTH, A_QBLOCK, gw), _F32),
            pltpu.VMEM((2, A_PIPE_WIDTH, A_QBLOCK, gw), _F32),
        ],
        compiler_params=pltpu.CompilerParams(
            dimension_semantics=("parallel", "arbitrary"), vmem_limit_bytes=VMEM_LIMIT),
        name="mixer_a",
    )(masks, q1, q2, q3, k1, k2, k3, v1, v2, v3)


def _na_col_start(j):
    return min(max(j * NA_QCOLS - NA_COLS // 2, 0), GRID_W - NA_KCOLS)


def _na_bias_table(rpb, rows):
    nrb = rows // NA_QROWS
    qr = np.arange(NA_QROWS)
    qc = np.arange(NA_QCOLS)
    kr = np.arange(NA_KROWS)
    kc = np.arange(NA_KCOLS)
    dr_all, okr_all = [], []
    for i in (0, 1, nrb - 1):
        r_abs = i * NA_QROWS + qr
        win_r = np.clip(r_abs - NA_ROWS // 2, 0, rows - NA_ROWS)
        k_abs = min(max(i * NA_QROWS - NA_ROWS // 2, 0), rows - NA_KROWS) + kr
        okr_all.append((k_abs[None, :] >= win_r[:, None]) & (k_abs[None, :] < win_r[:, None] + NA_ROWS))
        dr_all.append(np.clip(k_abs[None, :] - r_abs[:, None] + NA_ROWS - 1, 0, 2 * NA_ROWS - 2))
    dc_all, okc_all = [], []
    for j in range(GRID_W // NA_QCOLS):
        c_abs = j * NA_QCOLS + qc
        win_c = np.clip(c_abs - NA_COLS // 2, 0, GRID_W - NA_COLS)
        k_abs = _na_col_start(j) + kc
        okc_all.append((k_abs[None, :] >= win_c[:, None]) & (k_abs[None, :] < win_c[:, None] + NA_COLS))
        dc_all.append(np.clip(k_abs[None, :] - c_abs[:, None] + NA_COLS - 1, 0, 2 * NA_COLS - 2))
    dr = np.stack(dr_all)
    okr = np.stack(okr_all)
    dc = np.stack(dc_all)
    okc = np.stack(okc_all)
    x_r = np.repeat(kr, NA_KCOLS)
    x_c = np.tile(kc, NA_KROWS)
    dr_x = dr[:, :, x_r]
    dc_x = dc[:, :, x_c]
    ok = okr[:, :, x_r][:, None, None, :, None, :] & okc[:, :, x_c][None, :, None, None, :, :]
    csel = (jnp.asarray(dc_x, jnp.int32)[..., None] == jnp.arange(2 * NA_COLS - 1)).astype(_F32)
    cols = jnp.einsum("hab,cqxb->achqx", rpb.astype(_F32), csel, precision=lax.Precision.HIGHEST)
    bias = jnp.zeros((), _F32)
    for a in range(2 * NA_ROWS - 1):
        pick = jnp.asarray(dr_x == a)[:, None, None, :, None, :]
        bias = bias + jnp.where(pick, cols[a][None, :, :, None, :, :], 0.0)
    bias = jnp.where(jnp.asarray(ok), bias * LOG2E, NEG_INF)
    return bias.reshape(3, GRID_W // NA_QCOLS, B_HEADS, NA_QROWS * NA_QCOLS, NA_KROWS * NA_KCOLS)


def _attn_b_kernel(bias_ref, q_ref, k0_ref, k1_ref, k2_ref, k3_ref,
                   v0_ref, v1_ref, v2_ref, v3_ref, o_ref, s_scr, p_scr, inv_scr):
    k_refs = (k0_ref, k1_ref, k2_ref, k3_ref)
    v_refs = (v0_ref, v1_ref, v2_ref, v3_ref)
    nq = NA_QROWS * NA_QCOLS
    nh = A_HEADS_PER_GROUP
    half_w = nh * HEAD_DIM
    nhalves = B_HEADS // nh
    sel = _head_select(nq, half_w)
    left = lax.broadcasted_iota(jnp.int32, (nq, LANES), 1) < HEAD_DIM

    def lanes_of(b):
        hh = b % nhalves
        return slice(hh * half_w, (hh + 1) * half_w)

    def window(refs, b):
        cs = _na_col_start(b // nhalves)
        return jnp.concatenate(
            [refs[kr // NA_PIECE_ROWS][(kr % NA_PIECE_ROWS) * GRID_W + cs:
                                       (kr % NA_PIECE_ROWS) * GRID_W + cs + NA_KCOLS, lanes_of(b)]
             for kr in range(NA_KROWS)], axis=0)

    def scores(b, slot):
        j = b // nhalves
        q = jnp.concatenate(
            [q_ref[r * GRID_W + j * NA_QCOLS:r * GRID_W + (j + 1) * NA_QCOLS, lanes_of(b)]
             for r in range(NA_QROWS)], axis=0)
        qexp = jnp.where(sel, jnp.concatenate([q] * nh, axis=0), jnp.zeros((), _BF16))
        s_scr[slot] = _dot_nt(qexp, window(k_refs, b))

    def softmax(b, slot):
        j, hh = b // nhalves, b % nhalves
        dens = []
        for h in range(nh):
            rows = slice(h * nq, (h + 1) * nq)
            s = s_scr[slot, rows, :] + bias_ref[j, hh * nh + h]
            e = jnp.exp2(s - jnp.max(s, axis=-1, keepdims=True))
            dens.append(jnp.sum(e, axis=-1, keepdims=True))
            p_scr[slot, rows, :] = e.astype(_BF16)
        inv_scr[slot] = jnp.concatenate(
            [1.0 / jnp.where(left, dens[2 * c], dens[2 * c + 1]) for c in range(half_w // LANES)], axis=1)

    def output(b, slot):
        j = b // nhalves
        pv = _dot(p_scr[slot], window(v_refs, b))
        ob = (_diag_blocks(pv, nq, left) * inv_scr[slot]).astype(_BF16)
        for r in range(NA_QROWS):
            o_ref[r * GRID_W + j * NA_QCOLS:r * GRID_W + (j + 1) * NA_QCOLS, lanes_of(b)] = (
                ob[r * NA_QCOLS:(r + 1) * NA_QCOLS, :])

    total = (GRID_W // NA_QCOLS) * nhalves
    for it in range(total + 2):
        if it < total:
            scores(it, it % 2)
        if 1 <= it <= total:
            softmax(it - 1, (it - 1) % 2)
        if it >= 2:
            output(it - 2, it % 2)


def _mixer_b(qb, kb, vb, bias_table):
    bsz, seq, _ = qb.shape
    rows = seq // GRID_W
    nrb = rows // NA_QROWS
    q_tokens = NA_QROWS * GRID_W
    piece_tokens = NA_PIECE_ROWS * GRID_W
    npieces = NA_KROWS // NA_PIECE_ROWS
    last_piece = rows // NA_PIECE_ROWS - npieces

    def piece0(i):
        return jnp.clip(i * (NA_QROWS // NA_PIECE_ROWS) - 1, 0, last_piece)

    def row_case(i):
        return jnp.where(i == 0, 0, jnp.where(i == nrb - 1, 2, 1))

    bias_spec = pl.BlockSpec((None,) + bias_table.shape[1:], lambda b, i: (row_case(i), 0, 0, 0, 0))
    q_spec = pl.BlockSpec((None, q_tokens, B_WIDTH), lambda b, i: (b, i, 0))
    piece_specs = [pl.BlockSpec((None, piece_tokens, B_WIDTH), lambda b, i, n=n: (b, piece0(i) + n, 0))
                   for n in range(npieces)]
    return pl.pallas_call(
        _attn_b_kernel,
        grid=(bsz, nrb),
        in_specs=[bias_spec, q_spec] + piece_specs + piece_specs,
        out_specs=q_spec,
        out_shape=jax.ShapeDtypeStruct((bsz, seq, B_WIDTH), _BF16),
        scratch_shapes=[
            pltpu.VMEM((2, A_HEADS_PER_GROUP * NA_QROWS * NA_QCOLS, NA_KROWS * NA_KCOLS), _F32),
            pltpu.VMEM((2, A_HEADS_PER_GROUP * NA_QROWS * NA_QCOLS, NA_KROWS * NA_KCOLS), _BF16),
            pltpu.VMEM((2, NA_QROWS * NA_QCOLS, A_GROUP_WIDTH), _F32),
        ],
        compiler_params=pltpu.CompilerParams(
            dimension_semantics=("parallel", "arbitrary"), vmem_limit_bytes=VMEM_LIMIT),
        name="mixer_b",
    )(bias_table, qb, kb, kb, kb, kb, vb, vb, vb, vb)


def _layer_norm(h, g, b):
    mu = jnp.mean(h, axis=-1, keepdims=True)
    c = h - mu
    var = jnp.mean(c * c, axis=-1, keepdims=True)
    return c * lax.rsqrt(var + LN_EPS) * g + b


def _post_kernel(x_ref, ya_ref, yb_ref,
                 wg_ref, bg_ref, wa_ref, wb_ref, wo_ref, ln1g_ref, ln1b_ref,
                 w1_ref, b1_ref, w2_ref, b2_ref, ln2g_ref, ln2b_ref, y_ref, *, alpha):
    x = x_ref[...]
    xb = x.astype(_BF16)
    ya = _dot(ya_ref[...], wa_ref[...])
    yb = _dot(yb_ref[...], wb_ref[...])
    ga = jax.nn.sigmoid(_dot(xb, wg_ref[:, :D_MODEL]) + bg_ref[0:1, :])
    gb = jax.nn.sigmoid(_dot(xb, wg_ref[:, D_MODEL:]) + bg_ref[1:2, :])
    merged = ga * ya + gb * yb
    h = alpha * x + _dot(merged.astype(_BF16), wo_ref[...])
    x1 = _layer_norm(h, ln1g_ref[...], ln1b_ref[...])
    x1b = x1.astype(_BF16)
    h2 = alpha * x1 + b2_ref[...]
    for c in range(D_FF // FF_CHUNK):
        cols = slice(c * FF_CHUNK, (c + 1) * FF_CHUNK)
        hid = jnp.maximum(_dot(x1b, w1_ref[:, cols]) + b1_ref[:, cols], 0.0)
        h2 = h2 + _dot((hid * hid).astype(_BF16), w2_ref[cols, :])
    y_ref[...] = _layer_norm(h2, ln2g_ref[...], ln2b_ref[...])


def _post(x, ya, yb, p, alpha, tm=512):
    bsz, seq, d = x.shape
    tok = lambda w: pl.BlockSpec((None, tm, w), lambda b, t: (b, t, 0))
    weights = (p["w_gate"], p["b_gate"], p["w_a"], p["w_b"], p["w_o"], p["ln1_g"], p["ln1_b"],
               p["w_ff1"], p["b_ff1"], p["w_ff2"], p["b_ff2"], p["ln2_g"], p["ln2_b"])
    return pl.pallas_call(
        functools.partial(_post_kernel, alpha=alpha),
        grid=(bsz, seq // tm),
        in_specs=[tok(d), tok(A_GROUP_WIDTH), tok(B_WIDTH)] + [_resident(w.shape) for w in weights],
        out_specs=tok(d),
        out_shape=jax.ShapeDtypeStruct((bsz, seq, d), _F32),
        compiler_params=pltpu.CompilerParams(
            dimension_semantics=("parallel", "parallel"), vmem_limit_bytes=VMEM_LIMIT),
        name="post_ffn",
    )(x, ya, yb, *weights)


def _encoder_layer(x, p, consts, alpha):
    outs = _project(x, p["w_qkv"], consts["rope"])
    ya = _mixer_a(outs[:9], consts["a_masks"])
    yb = _mixer_b(*outs[9:], p["na_bias"])
    return _post(x, ya, yb, p, alpha)


def kernel(x_prompt, x_sample, w_in, b_gate, w_branch_a, w_branch_b, w_out, rel_pos_bias,
           ln1_g, ln1_b, w_ff1, b_ff1, w_ff2, b_ff2, ln2_g, ln2_b):
    depth = w_in.shape[0]
    alpha = (2.0 * depth) ** 0.25
    seq = x_prompt.shape[1]
    assert x_sample.shape[1] == seq and seq % (GRID_W * NA_KROWS) == 0
    consts = {"rope": _rope_tables(seq), "a_masks": _a_masks()}
    scale = HEAD_DIM ** -0.5 * LOG2E
    col_scale = np.ones((QKV_COLS,), np.float32)
    col_scale[0:A_WIDTH] = scale
    col_scale[3 * A_WIDTH:3 * A_WIDTH + B_WIDTH] = scale
    row = lambda v: v.reshape(1, -1)
    y_prompt, y_sample = x_prompt, x_sample
    for layer in range(depth):
        p = {
            "w_qkv": (w_in[layer][:, :QKV_COLS] * col_scale).astype(_BF16),
            "w_gate": w_in[layer][:, QKV_COLS:].astype(_BF16),
            "b_gate": b_gate[layer],
            "w_a": w_branch_a[layer].astype(_BF16),
            "w_b": w_branch_b[layer].astype(_BF16),
            "w_o": w_out[layer].astype(_BF16),
            "na_bias": _na_bias_table(rel_pos_bias[layer], seq // GRID_W),
            "ln1_g": row(ln1_g[layer]), "ln1_b": row(ln1_b[layer]),
            "w_ff1": w_ff1[layer].astype(_BF16), "b_ff1": row(b_ff1[layer]),
            "w_ff2": w_ff2[layer].astype(_BF16), "b_ff2": row(b_ff2[layer]),
            "ln2_g": row(ln2_g[layer]), "ln2_b": row(ln2_b[layer]),
        }
        y_prompt = _encoder_layer(y_prompt, p, consts, alpha)
        y_sample = _encoder_layer(y_sample, p, consts, alpha)
    return (y_prompt, y_sample)
```

```python
import functools
import math

import numpy as np
import jax
import jax.numpy as jnp
from jax import lax
from jax.experimental import pallas as pl
from jax.experimental.pallas import tpu as pltpu

D_MODEL = 1024
HEAD_DIM = 64
A_GROUPS = ((128, 1), (512, 4), (2048, 16))
A_DILATIONS = tuple(d for _, d in A_GROUPS)
A_HEADS_PER_GROUP = 4
A_GROUP_WIDTH = A_HEADS_PER_GROUP * HEAD_DIM
A_WIDTH = len(A_GROUPS) * A_GROUP_WIDTH
A_SIDE = 64
A_QBLOCK = 128
A_KWIN = A_QBLOCK + 2 * A_SIDE
A_TILE = A_QBLOCK * max(A_DILATIONS)
A_PIPE_WIDTH = 2
B_HEADS = 8
B_WIDTH = B_HEADS * HEAD_DIM
GRID_W = 64
NA_ROWS = 8
NA_COLS = 16
NA_QROWS = 8
NA_QCOLS = 16
NA_KROWS = 16
NA_KCOLS = 32
NA_PIECE_ROWS = 4
ROPE_THETA = 500000.0
ROPE_DIMS = HEAD_DIM // 4
D_FF = 4 * D_MODEL
FF_CHUNK = 1024
QKV_COLS = 3 * A_WIDTH + 3 * B_WIDTH
LN_EPS = 1e-5
NEG_INF = -1e30
LOG2E = math.log2(math.e)
LANES = 128

VMEM_LIMIT = 56 * 1024 * 1024

_BF16 = jnp.bfloat16
_F32 = jnp.float32


def _dot(a, b):
    return jnp.dot(a, b, preferred_element_type=_F32)


def _dot_nt(a, b):
    return lax.dot_general(a, b, (((1,), (1,)), ((), ())), preferred_element_type=_F32)


def _resident(shape):
    nd = len(shape)
    return pl.BlockSpec(shape, lambda *_: (0,) * nd, pipeline_mode=pl.Buffered(1))


def _head_select(rows_per_head, width):
    n = A_HEADS_PER_GROUP * rows_per_head
    rh = lax.broadcasted_iota(jnp.int32, (n, width), 0) // rows_per_head
    lh = lax.broadcasted_iota(jnp.int32, (n, width), 1) // HEAD_DIM
    return rh == lh


def _diag_blocks(t, nq, left):
    cols = (lambda a, c: a) if t.shape[1] == 1 else (lambda a, c: a[:, c * LANES:(c + 1) * LANES])
    halves = [jnp.where(left, cols(t[(2 * c) * nq:(2 * c + 1) * nq], c), cols(t[(2 * c + 1) * nq:(2 * c + 2) * nq], c))
              for c in range(A_GROUP_WIDTH // LANES)]
    return jnp.concatenate(halves, axis=1)


def _proj_kernel(x_ref, w_ref, cos_ref, sa_ref, sb_ref,
                 q1_ref, k1_ref, v1_ref, q2_ref, k2_ref, v2_ref, q3_ref, k3_ref, v3_ref,
                 qb_ref, kb_ref, vb_ref, scr_ref):
    tm = x_ref.shape[0]
    xb = x_ref[...].astype(_BF16)
    cos = cos_ref[...]
    sa = sa_ref[...]
    sb = sb_ref[...]

    def rope(acc):
        outs = []
        for c in range(A_WIDTH // LANES):
            ch = acc[:, c * LANES:(c + 1) * LANES]
            up = pltpu.roll(ch, LANES - ROPE_DIMS // 2, 1)
            dn = pltpu.roll(ch, ROPE_DIMS // 2, 1)
            outs.append(ch * cos + up * sa + dn * sb)
        return jnp.concatenate(outs, axis=1)

    def emit(acc, outs, slot):
        gw = A_GROUP_WIDTH
        outs[0][...] = acc[:, 0:gw].astype(_BF16)
        for g in (1, 2):
            d = A_DILATIONS[g]
            for c in range(gw // LANES):
                buf = scr_ref.at[(2 * slot + g - 1) * (gw // LANES) + c]
                buf[...] = acc[:, g * gw + c * LANES:g * gw + (c + 1) * LANES]
                for r in range(d):
                    outs[g][r, :, c * LANES:(c + 1) * LANES] = buf[pl.ds(r, tm // d, stride=d), :].astype(_BF16)

    a = A_WIDTH
    c = B_WIDTH
    emit(rope(_dot(xb, w_ref[:, 0:a])), (q1_ref, q2_ref, q3_ref), 0)
    emit(rope(_dot(xb, w_ref[:, a:2 * a])), (k1_ref, k2_ref, k3_ref), 1)
    emit(_dot(xb, w_ref[:, 2 * a:3 * a]), (v1_ref, v2_ref, v3_ref), 2)
    qb_ref[...] = _dot(xb, w_ref[:, 3 * a:3 * a + c]).astype(_BF16)
    kb_ref[...] = _dot(xb, w_ref[:, 3 * a + c:3 * a + 2 * c]).astype(_BF16)
    vb_ref[...] = _dot(xb, w_ref[:, 3 * a + 2 * c:3 * a + 3 * c]).astype(_BF16)


def _rope_tables(seq):
    half = ROPE_DIMS // 2
    inv = ROPE_THETA ** (-jnp.arange(half, dtype=_F32) / half)
    ang = jnp.arange(seq, dtype=_F32)[:, None] * inv[None, :]
    cos = jnp.cos(ang)
    sin = jnp.sin(ang)
    ones = jnp.ones((seq, HEAD_DIM - ROPE_DIMS), _F32)
    zeros = jnp.zeros((seq, HEAD_DIM - ROPE_DIMS), _F32)
    zh = jnp.zeros((seq, half), _F32)
    cos_h = jnp.concatenate([cos, cos, ones], axis=1)
    sa_h = jnp.concatenate([-sin, zh, zeros], axis=1)
    sb_h = jnp.concatenate([zh, sin, zeros], axis=1)
    rep = LANES // HEAD_DIM
    return (jnp.tile(cos_h, (1, rep)), jnp.tile(sa_h, (1, rep)), jnp.tile(sb_h, (1, rep)))


def _project(x, w_qkv, tables, tm=1024):
    bsz, seq, d = x.shape
    cos, sa, sb = tables
    grid = (bsz, seq // tm)
    tok = lambda w: pl.BlockSpec((None, tm, w), lambda b, t: (b, t, 0))
    tab = pl.BlockSpec((tm, LANES), lambda b, t: (t, 0))
    gw = A_GROUP_WIDTH
    a_specs, a_shapes = [], []
    for dil in A_DILATIONS:
        if dil == 1:
            a_specs.append(tok(gw))
            a_shapes.append(jax.ShapeDtypeStruct((bsz, seq, gw), _BF16))
        else:
            a_specs.append(pl.BlockSpec((None, dil, tm // dil, gw), lambda b, t: (b, 0, t, 0)))
            a_shapes.append(jax.ShapeDtypeStruct((bsz, dil, seq // dil, gw), _BF16))
    b_shape = jax.ShapeDtypeStruct((bsz, seq, B_WIDTH), _BF16)
    out_specs = [a_specs[g] for g in range(3) for _ in range(3)] + [tok(B_WIDTH)] * 3
    out_shape = [a_shapes[g] for g in range(3) for _ in range(3)] + [b_shape] * 3
    return pl.pallas_call(
        _proj_kernel,
        grid=grid,
        in_specs=[tok(d), _resident(w_qkv.shape), tab, tab, tab],
        out_specs=out_specs,
        out_shape=out_shape,
        scratch_shapes=[pltpu.VMEM((6 * gw // LANES, tm, LANES), _F32)],
        compiler_params=pltpu.CompilerParams(
            dimension_semantics=("parallel", "parallel"), vmem_limit_bytes=VMEM_LIMIT),
        name="proj_rope",
    )(x, w_qkv, cos, sa, sb)


def _mixer_a_kernel(mask_ref, q1_ref, q2_ref, q3_ref, *rest, seq):
    ngroups = len(A_GROUPS)
    k_parts, v_parts = rest[:3 * ngroups], rest[3 * ngroups:6 * ngroups]
    y_ref = rest[6 * ngroups]
    o2_scr, l2_scr, o3_scr, l3_scr, s_scr, p_scr, inv_scr, l1_scr = rest[6 * ngroups + 1:6 * ngroups + 9]
    k_refs = rest[6 * ngroups + 9:7 * ngroups + 9]
    v_refs = rest[7 * ngroups + 9:]
    t = pl.program_id(1)
    qb = A_QBLOCK
    nh = A_HEADS_PER_GROUP
    nblk = A_TILE // qb
    sel = _head_select(qb, A_GROUP_WIDTH)
    left = lax.broadcasted_iota(jnp.int32, (qb, LANES), 1) < HEAD_DIM
    nhalf = A_GROUP_WIDTH // LANES
    q_refs = (q1_ref, q2_ref, q3_ref)

    for parts, bufs in ((k_parts, k_refs), (v_parts, v_refs)):
        for g in range(ngroups):
            cur, lo, hi = parts[3 * g:3 * g + 3]
            rows = cur.shape[1]
            bufs[g][:, 0:A_SIDE, :] = lo[...]
            bufs[g][:, A_SIDE:A_SIDE + rows, :] = cur[...]
            bufs[g][:, A_SIDE + rows:, :] = hi[...]

    def place(g, n):
        d = A_DILATIONS[g]
        per_res = nblk // d
        r, i = n // per_res, n % per_res
        sub_len = seq // d
        l0 = t * (A_TILE // d) + i * qb
        case = jnp.where(l0 == 0, 0, jnp.where(l0 == sub_len - qb, 2, 1))
        row0 = i * qb
        if not isinstance(row0, int):
            row0 = pl.multiple_of(row0, qb)
        return r, row0, row0, case

    def plane(refs, g, r):
        return refs[g].at[r]

    def scores(g, n, slot):
        r, row0, ks, _ = place(g, n)
        q = plane(q_refs, g, r)[pl.ds(row0, qb), :]
        qexp = jnp.where(sel, jnp.concatenate([q] * nh, axis=0), jnp.zeros((), _BF16))
        s_scr[slot] = _dot_nt(qexp, plane(k_refs, g, r)[pl.ds(ks, A_KWIN), :])

    def softmax(g, n, slot):
        r, row0, _, case = place(g, n)
        mask = mask_ref[case]
        maxes, dens = [], []
        s_blk = s_scr.at[slot]
        p_blk = p_scr.at[slot]
        for h in range(nh):
            rows = slice(h * qb, (h + 1) * qb)
            s = s_blk[rows, :] + mask
            m = jnp.max(s, axis=-1, keepdims=True)
            e = jnp.exp2(s - m)
            p_blk[rows, :] = e.astype(_BF16)
            maxes.append(m)
            dens.append(jnp.sum(e, axis=-1, keepdims=True))
        pair = lambda v, c: jnp.where(left, v[2 * c], v[2 * c + 1])
        den = [pair(dens, c) for c in range(nhalf)]
        inv_scr[slot] = jnp.concatenate([1.0 / dc for dc in den], axis=1)
        lse = jnp.concatenate([pair(maxes, c) + jnp.log2(den[c]) for c in range(nhalf)], axis=1)
        if g == 0:
            l1_scr[slot] = lse
        else:
            d = A_DILATIONS[g]
            scatter((l2_scr, l3_scr)[g - 1], row0 * d + r, d, lse)

    def output(g, n, slot):
        r, row0, ks, _ = place(g, n)
        vw = plane(v_refs, g, r)[pl.ds(ks, A_KWIN), :]
        p_blk = p_scr.at[slot]
        halves = []
        for c in range(nhalf):
            pv = _dot(p_blk[2 * c * qb:2 * (c + 1) * qb, :], vw[:, c * LANES:(c + 1) * LANES])
            halves.append(jnp.where(left, pv[:qb], pv[qb:]))
        o = jnp.concatenate(halves, axis=1) * inv_scr[slot]
        if g != 0:
            d = A_DILATIONS[g]
            scatter((o2_scr, o3_scr)[g - 1], row0 * d + r, d, o)
            return
        rows = pl.ds(row0, qb)
        l1 = l1_scr[slot]
        l2 = gather(l2_scr, rows)
        l3 = gather(l3_scr, rows)
        lm = jnp.maximum(jnp.maximum(l1, l2), l3)
        e1 = jnp.exp2(l1 - lm)
        e2 = jnp.exp2(l2 - lm)
        e3 = jnp.exp2(l3 - lm)
        ya = (e1 * o + e2 * gather(o2_scr, rows) + e3 * gather(o3_scr, rows)) * (1.0 / (e1 + e2 + e3))
        y_ref[rows, :] = ya.astype(_BF16)

    def scatter(scr, start, stride, val):
        for c in range(nhalf):
            scr[c, pl.ds(start, qb, stride=stride), :] = val[:, c * LANES:(c + 1) * LANES]

    def gather(scr, rows):
        return jnp.concatenate([scr[c, rows, :] for c in range(nhalf)], axis=1)

    order = (2, 1, 0)
    wide = A_PIPE_WIDTH
    steps_per_group = nblk // wide
    total = steps_per_group * len(order)

    def step(k, g_of, local_of, parity):
        for w in range(wide):
            if 0 <= k < total:
                output(g_of(k), wide * local_of(k) + w, (parity, w))
        for w in range(wide):
            if k + 2 < total:
                scores(g_of(k + 2), wide * local_of(k + 2) + w, (parity, w))
        for w in range(wide):
            if 0 <= k + 1 < total:
                softmax(g_of(k + 1), wide * local_of(k + 1) + w, (1 - parity, w))

    static_g = lambda j: order[j // steps_per_group]
    static_local = lambda j: j % steps_per_group
    step(-2, static_g, static_local, 0)
    step(-1, static_g, static_local, 1)
    for gi, g in enumerate(order):
        base = gi * steps_per_group

        def steady(m, carry, g=g, base=base):
            for par in (0, 1):
                step(base + par, lambda j: g, lambda j: 2 * m + (j - base), par)
            return carry

        lax.fori_loop(0, (steps_per_group - 2) // 2, steady, 0)
        for k in (base + steps_per_group - 2, base + steps_per_group - 1):
            step(k, static_g, static_local, k % 2)


def _a_masks():
    qi = np.arange(A_QBLOCK)[:, None]
    kj = np.arange(A_KWIN)[None, :]
    band = np.abs(kj - A_SIDE - qi) <= A_SIDE
    cases = (band & (kj >= A_SIDE), band, band & (kj < A_KWIN - A_SIDE))
    return jnp.asarray(np.stack([np.where(ok, 0.0, NEG_INF) for ok in cases]), _F32)


def _mixer_a(qkv, masks):
    bsz, seq, gw = qkv[0].shape
    assert seq % A_TILE == 0 and seq // max(A_DILATIONS) >= A_KWIN
    q1, k1, v1, q2, k2, v2, q3, k3, v3 = [a.reshape(bsz, -1, a.shape[-2], gw) for a in qkv]

    def tile_spec(dil):
        return pl.BlockSpec((None, dil, A_TILE // dil, gw), lambda b, t: (b, 0, t, 0))

    def halo_specs(dil):
        per_tile = A_TILE // dil // A_SIDE
        last = seq // dil // A_SIDE - 1
        lo = pl.BlockSpec((None, dil, A_SIDE, gw), lambda b, t: (b, 0, jnp.maximum(t * per_tile - 1, 0), 0))
        hi = pl.BlockSpec((None, dil, A_SIDE, gw), lambda b, t: (b, 0, jnp.minimum((t + 1) * per_tile, last), 0))
        return [lo, hi]

    qs = [tile_spec(d) for d in A_DILATIONS]
    kvs = [s for d in A_DILATIONS for s in [tile_spec(d)] + halo_specs(d)]
    kv_bufs = [pltpu.VMEM((d, A_TILE // d + 2 * A_SIDE, gw), _BF16) for d in A_DILATIONS]
    return pl.pallas_call(
        functools.partial(_mixer_a_kernel, seq=seq),
        grid=(bsz, seq // A_TILE),
        in_specs=[_resident(masks.shape)] + qs + kvs + kvs,
        out_specs=pl.BlockSpec((None, A_TILE, gw), lambda b, t: (b, t, 0)),
        out_shape=jax.ShapeDtypeStruct((bsz, seq, gw), _BF16),
        scratch_shapes=[pltpu.VMEM((gw // LANES, A_TILE, LANES), _F32)] * 4 + [
            pltpu.VMEM((2, A_PIPE_WIDTH, A_HEADS_PER_GROUP * A_QBLOCK, A_KWIN), _F32),
            pltpu.VMEM((2, A_PIPE_WIDTH, A_HEADS_PER_GROUP * A_QBLOCK, A_KWIN), _BF16),
            pltpu.VMEM((2, A_PIPE_WIDTH, A_QBLOCK, gw), _F32),
            pltpu.VMEM((2, A_PIPE_WIDTH, A_QBLOCK, gw), _F32),
        ] + kv_bufs + kv_bufs,
        compiler_params=pltpu.CompilerParams(
            dimension_semantics=("parallel", "arbitrary"), vmem_limit_bytes=VMEM_LIMIT),
        name="mixer_a",
    )(masks, q1, q2, q3, *[a for a in (k1, k2, k3) for _ in range(3)], *[a for a in (v1, v2, v3) for _ in range(3)])


def _na_col_start(j):
    return min(max(j * NA_QCOLS - NA_COLS // 2, 0), GRID_W - NA_KCOLS)


def _na_bias_table(rpb, rows):
    nrb = rows // NA_QROWS
    qr = np.arange(NA_QROWS)
    qc = np.arange(NA_QCOLS)
    kr = np.arange(NA_KROWS)
    kc = np.arange(NA_KCOLS)
    dr_all, okr_all = [], []
    for i in (0, 1, nrb - 1):
        r_abs = i * NA_QROWS + qr
        win_r = np.clip(r_abs - NA_ROWS // 2, 0, rows - NA_ROWS)
        k_abs = min(max(i * NA_QROWS - NA_ROWS // 2, 0), rows - NA_KROWS) + kr
        okr_all.append((k_abs[None, :] >= win_r[:, None]) & (k_abs[None, :] < win_r[:, None] + NA_ROWS))
        dr_all.append(np.clip(k_abs[None, :] - r_abs[:, None] + NA_ROWS - 1, 0, 2 * NA_ROWS - 2))
    dc_all, okc_all = [], []
    for j in range(GRID_W // NA_QCOLS):
        c_abs = j * NA_QCOLS + qc
        win_c = np.clip(c_abs - NA_COLS // 2, 0, GRID_W - NA_COLS)
        k_abs = _na_col_start(j) + kc
        okc_all.append((k_abs[None, :] >= win_c[:, None]) & (k_abs[None, :] < win_c[:, None] + NA_COLS))
        dc_all.append(np.clip(k_abs[None, :] - c_abs[:, None] + NA_COLS - 1, 0, 2 * NA_COLS - 2))
    dr = np.stack(dr_all)
    okr = np.stack(okr_all)
    dc = np.stack(dc_all)
    okc = np.stack(okc_all)
    x_r = np.repeat(kr, NA_KCOLS)
    x_c = np.tile(kc, NA_KROWS)
    dr_x = dr[:, :, x_r]
    dc_x = dc[:, :, x_c]
    ok = okr[:, :, x_r][:, None, None, :, None, :] & okc[:, :, x_c][None, :, None, None, :, :]
    csel = (jnp.asarray(dc_x, jnp.int32)[..., None] == jnp.arange(2 * NA_COLS - 1)).astype(_F32)
    cols = jnp.einsum("hab,cqxb->achqx", rpb.astype(_F32), csel, precision=lax.Precision.HIGHEST)
    bias = jnp.zeros((), _F32)
    for a in range(2 * NA_ROWS - 1):
        pick = jnp.asarray(dr_x == a)[:, None, None, :, None, :]
        bias = bias + jnp.where(pick, cols[a][None, :, :, None, :, :], 0.0)
    bias = jnp.where(jnp.asarray(ok), bias * LOG2E, NEG_INF)
    return bias.reshape(3, GRID_W // NA_QCOLS, B_HEADS, NA_QROWS * NA_QCOLS, NA_KROWS * NA_KCOLS)


def _attn_b_kernel(bias_ref, q_ref, k0_ref, k1_ref, k2_ref, k3_ref,
                   v0_ref, v1_ref, v2_ref, v3_ref, o_ref, s_scr, p_scr, inv_scr):
    k_refs = (k0_ref, k1_ref, k2_ref, k3_ref)
    v_refs = (v0_ref, v1_ref, v2_ref, v3_ref)
    nq = NA_QROWS * NA_QCOLS
    nh = A_HEADS_PER_GROUP
    half_w = nh * HEAD_DIM
    nhalves = B_HEADS // nh
    sel = _head_select(nq, half_w)
    left = lax.broadcasted_iota(jnp.int32, (nq, LANES), 1) < HEAD_DIM

    def lanes_of(b):
        hh = b % nhalves
        return slice(hh * half_w, (hh + 1) * half_w)

    def window(refs, b):
        cs = _na_col_start(b // nhalves)
        return jnp.concatenate(
            [refs[kr // NA_PIECE_ROWS][(kr % NA_PIECE_ROWS) * GRID_W + cs:
                                       (kr % NA_PIECE_ROWS) * GRID_W + cs + NA_KCOLS, lanes_of(b)]
             for kr in range(NA_KROWS)], axis=0)

    def scores(b, slot):
        j = b // nhalves
        q = jnp.concatenate(
            [q_ref[r * GRID_W + j * NA_QCOLS:r * GRID_W + (j + 1) * NA_QCOLS, lanes_of(b)]
             for r in range(NA_QROWS)], axis=0)
        qexp = jnp.where(sel, jnp.concatenate([q] * nh, axis=0), jnp.zeros((), _BF16))
        s_scr[slot] = _dot_nt(qexp, window(k_refs, b))

    def softmax(b, slot):
        j, hh = b // nhalves, b % nhalves
        dens = []
        for h in range(nh):
            rows = slice(h * nq, (h + 1) * nq)
            s = s_scr[slot, rows, :] + bias_ref[j, hh * nh + h]
            e = jnp.exp2(s - jnp.max(s, axis=-1, keepdims=True))
            dens.append(jnp.sum(e, axis=-1, keepdims=True))
            p_scr[slot, rows, :] = e.astype(_BF16)
        inv_scr[slot] = jnp.concatenate(
            [1.0 / jnp.where(left, dens[2 * c], dens[2 * c + 1]) for c in range(half_w // LANES)], axis=1)

    def output(b, slot):
        j = b // nhalves
        pv = _dot(p_scr[slot], window(v_refs, b))
        ob = (_diag_blocks(pv, nq, left) * inv_scr[slot]).astype(_BF16)
        for r in range(NA_QROWS):
            o_ref[r * GRID_W + j * NA_QCOLS:r * GRID_W + (j + 1) * NA_QCOLS, lanes_of(b)] = (
                ob[r * NA_QCOLS:(r + 1) * NA_QCOLS, :])

    total = (GRID_W // NA_QCOLS) * nhalves
    for it in range(total + 2):
        if it < total:
            scores(it, it % 2)
        if 1 <= it <= total:
            softmax(it - 1, (it - 1) % 2)
        if it >= 2:
            output(it - 2, it % 2)


def _mixer_b(qb, kb, vb, bias_table):
    bsz, seq, _ = qb.shape
    rows = seq // GRID_W
    nrb = rows // NA_QROWS
    q_tokens = NA_QROWS * GRID_W
    piece_tokens = NA_PIECE_ROWS * GRID_W
    npieces = NA_KROWS // NA_PIECE_ROWS
    last_piece = rows // NA_PIECE_ROWS - npieces

    def piece0(i):
        return jnp.clip(i * (NA_QROWS // NA_PIECE_ROWS) - 1, 0, last_piece)

    def row_case(i):
        return jnp.where(i == 0, 0, jnp.where(i == nrb - 1, 2, 1))

    bias_spec = pl.BlockSpec((None,) + bias_table.shape[1:], lambda b, i: (row_case(i), 0, 0, 0, 0))
    q_spec = pl.BlockSpec((None, q_tokens, B_WIDTH), lambda b, i: (b, i, 0))
    piece_specs = [pl.BlockSpec((None, piece_tokens, B_WIDTH), lambda b, i, n=n: (b, piece0(i) + n, 0))
                   for n in range(npieces)]
    return pl.pallas_call(
        _attn_b_kernel,
        grid=(bsz, nrb),
        in_specs=[bias_spec, q_spec] + piece_specs + piece_specs,
        out_specs=q_spec,
        out_shape=jax.ShapeDtypeStruct((bsz, seq, B_WIDTH), _BF16),
        scratch_shapes=[
            pltpu.VMEM((2, A_HEADS_PER_GROUP * NA_QROWS * NA_QCOLS, NA_KROWS * NA_KCOLS), _F32),
            pltpu.VMEM((2, A_HEADS_PER_GROUP * NA_QROWS * NA_QCOLS, NA_KROWS * NA_KCOLS), _BF16),
            pltpu.VMEM((2, NA_QROWS * NA_QCOLS, A_GROUP_WIDTH), _F32),
        ],
        compiler_params=pltpu.CompilerParams(
            dimension_semantics=("parallel", "arbitrary"), vmem_limit_bytes=VMEM_LIMIT),
        name="mixer_b",
    )(bias_table, qb, kb, kb, kb, kb, vb, vb, vb, vb)


def _layer_norm(h, g, b):
    mu = jnp.mean(h, axis=-1, keepdims=True)
    c = h - mu
    var = jnp.mean(c * c, axis=-1, keepdims=True)
    return c * lax.rsqrt(var + LN_EPS) * g + b


def _post_kernel(x_ref, ya_ref, yb_ref,
                 wg_ref, bg_ref, wa_ref, wb_ref, wo_ref, ln1g_ref, ln1b_ref,
                 w1_ref, b1_ref, w2_ref, b2_ref, ln2g_ref, ln2b_ref, y_ref, *, alpha):
    x = x_ref[...]
    xb = x.astype(_BF16)
    ya = _dot(ya_ref[...], wa_ref[...])
    yb = _dot(yb_ref[...], wb_ref[...])
    ga = jax.nn.sigmoid(_dot(xb, wg_ref[:, :D_MODEL]) + bg_ref[0:1, :])
    gb = jax.nn.sigmoid(_dot(xb, wg_ref[:, D_MODEL:]) + bg_ref[1:2, :])
    merged = ga * ya + gb * yb
    h = alpha * x + _dot(merged.astype(_BF16), wo_ref[...])
    x1 = _layer_norm(h, ln1g_ref[...], ln1b_ref[...])
    x1b = x1.astype(_BF16)
    h2 = alpha * x1 + b2_ref[...]
    for c in range(D_FF // FF_CHUNK):
        cols = slice(c * FF_CHUNK, (c + 1) * FF_CHUNK)
        hid = jnp.maximum(_dot(x1b, w1_ref[:, cols]) + b1_ref[:, cols], 0.0)
        h2 = h2 + _dot((hid * hid).astype(_BF16), w2_ref[cols, :])
    y_ref[...] = _layer_norm(h2, ln2g_ref[...], ln2b_ref[...])


def _post(x, ya, yb, p, alpha, tm=512):
    bsz, seq, d = x.shape
    tok = lambda w: pl.BlockSpec((None, tm, w), lambda b, t: (b, t, 0))
    weights = (p["w_gate"], p["b_gate"], p["w_a"], p["w_b"], p["w_o"], p["ln1_g"], p["ln1_b"],
               p["w_ff1"], p["b_ff1"], p["w_ff2"], p["b_ff2"], p["ln2_g"], p["ln2_b"])
    return pl.pallas_call(
        functools.partial(_post_kernel, alpha=alpha),
        grid=(bsz, seq // tm),
        in_specs=[tok(d), tok(A_GROUP_WIDTH), tok(B_WIDTH)] + [_resident(w.shape) for w in weights],
        out_specs=tok(d),
        out_shape=jax.ShapeDtypeStruct((bsz, seq, d), _F32),
        compiler_params=pltpu.CompilerParams(
            dimension_semantics=("parallel", "parallel"), vmem_limit_bytes=VMEM_LIMIT),
        name="post_ffn",
    )(x, ya, yb, *weights)


def _encoder_layer(x, p, consts, alpha):
    outs = _project(x, p["w_qkv"], consts["rope"])
    ya = _mixer_a(outs[:9], consts["a_masks"])
    yb = _mixer_b(*outs[9:], p["na_bias"])
    return _post(x, ya, yb, p, alpha)


def kernel(x_prompt, x_sample, w_in, b_gate, w_branch_a, w_branch_b, w_out, rel_pos_bias,
           ln1_g, ln1_b, w_ff1, b_ff1, w_ff2, b_ff2, ln2_g, ln2_b):
    depth = w_in.shape[0]
    alpha = (2.0 * depth) ** 0.25
    seq = x_prompt.shape[1]
    assert x_sample.shape[1] == seq and seq % (GRID_W * NA_KROWS) == 0
    consts = {"rope": _rope_tables(seq), "a_masks": _a_masks()}
    scale = HEAD_DIM ** -0.5 * LOG2E
    col_scale = np.ones((QKV_COLS,), np.float32)
    col_scale[0:A_WIDTH] = scale
    col_scale[3 * A_WIDTH:3 * A_WIDTH + B_WIDTH] = scale
    row = lambda v: v.reshape(1, -1)
    y_prompt, y_sample = x_prompt, x_sample
    for layer in range(depth):
        p = {
            "w_qkv": (w_in[layer][:, :QKV_COLS] * col_scale).astype(_BF16),
            "w_gate": w_in[layer][:, QKV_COLS:].astype(_BF16),
            "b_gate": b_gate[layer],
            "w_a": w_branch_a[layer].astype(_BF16),
            "w_b": w_branch_b[layer].astype(_BF16),
            "w_o": w_out[layer].astype(_BF16),
            "na_bias": _na_bias_table(rel_pos_bias[layer], seq // GRID_W),
            "ln1_g": row(ln1_g[layer]), "ln1_b": row(ln1_b[layer]),
            "w_ff1": w_ff1[layer].astype(_BF16), "b_ff1": row(b_ff1[layer]),
            "w_ff2": w_ff2[layer].astype(_BF16), "b_ff2": row(b_ff2[layer]),
            "ln2_g": row(ln2_g[layer]), "ln2_b": row(ln2_b[layer]),
        }
        y_prompt = _encoder_layer(y_prompt, p, consts, alpha)
        y_sample = _encoder_layer(y_sample, p, consts, alpha)
    return (y_prompt, y_sample)
```

```python
import functools
import math

import numpy as np
import jax
import jax.numpy as jnp
from jax import lax
from jax.experimental import pallas as pl
from jax.experimental.pallas import tpu as pltpu

D_MODEL = 1024
HEAD_DIM = 64
A_GROUPS = ((128, 1), (512, 4), (2048, 16))
A_DILATIONS = tuple(d for _, d in A_GROUPS)
A_HEADS_PER_GROUP = 4
A_GROUP_WIDTH = A_HEADS_PER_GROUP * HEAD_DIM
A_WIDTH = len(A_GROUPS) * A_GROUP_WIDTH
A_SIDE = 64
A_QBLOCK = 128
A_KWIN = A_QBLOCK + 2 * A_SIDE
A_TILE = A_QBLOCK * max(A_DILATIONS)
A_PIPE_WIDTH = 2
A_SCRATCH_PITCH = (1, 4, 20)
B_HEADS = 8
B_WIDTH = B_HEADS * HEAD_DIM
GRID_W = 64
NA_ROWS = 8
NA_COLS = 16
NA_QROWS = 8
NA_QCOLS = 16
NA_KROWS = 16
NA_KCOLS = 32
NA_PIECE_ROWS = 4
ROPE_THETA = 500000.0
ROPE_DIMS = HEAD_DIM // 4
D_FF = 4 * D_MODEL
FF_CHUNK = 1024
QKV_COLS = 3 * A_WIDTH + 3 * B_WIDTH
LN_EPS = 1e-5
NEG_INF = -1e30
LOG2E = math.log2(math.e)
LANES = 128

VMEM_LIMIT = 56 * 1024 * 1024

_BF16 = jnp.bfloat16
_F32 = jnp.float32


def _dot(a, b):
    return jnp.dot(a, b, preferred_element_type=_F32)


def _dot_nt(a, b):
    return lax.dot_general(a, b, (((1,), (1,)), ((), ())), preferred_element_type=_F32)


def _resident(shape):
    nd = len(shape)
    return pl.BlockSpec(shape, lambda *_: (0,) * nd, pipeline_mode=pl.Buffered(1))


def _head_select(rows_per_head, width):
    n = A_HEADS_PER_GROUP * rows_per_head
    rh = lax.broadcasted_iota(jnp.int32, (n, width), 0) // rows_per_head
    lh = lax.broadcasted_iota(jnp.int32, (n, width), 1) // HEAD_DIM
    return rh == lh


def _diag_blocks(t, nq, left):
    cols = (lambda a, c: a) if t.shape[1] == 1 else (lambda a, c: a[:, c * LANES:(c + 1) * LANES])
    halves = [jnp.where(left, cols(t[(2 * c) * nq:(2 * c + 1) * nq], c), cols(t[(2 * c + 1) * nq:(2 * c + 2) * nq], c))
              for c in range(A_GROUP_WIDTH // LANES)]
    return jnp.concatenate(halves, axis=1)


def _proj_kernel(x_ref, w_ref, cos_ref, sa_ref, sb_ref,
                 q1_ref, k1_ref, v1_ref, q2_ref, k2_ref, v2_ref, q3_ref, k3_ref, v3_ref,
                 qb_ref, kb_ref, vb_ref, scr_ref):
    tm = x_ref.shape[0]
    xb = x_ref[...].astype(_BF16)
    cos = cos_ref[...]
    sa = sa_ref[...]
    sb = sb_ref[...]

    def rope(acc):
        outs = []
        for c in range(A_WIDTH // LANES):
            ch = acc[:, c * LANES:(c + 1) * LANES]
            up = pltpu.roll(ch, LANES - ROPE_DIMS // 2, 1)
            dn = pltpu.roll(ch, ROPE_DIMS // 2, 1)
            outs.append(ch * cos + up * sa + dn * sb)
        return jnp.concatenate(outs, axis=1)

    def emit(acc, outs, slot):
        gw = A_GROUP_WIDTH
        outs[0][...] = acc[:, 0:gw].astype(_BF16)
        for g in (1, 2):
            d = A_DILATIONS[g]
            for c in range(gw // LANES):
                buf = scr_ref.at[(2 * slot + g - 1) * (gw // LANES) + c]
                buf[...] = acc[:, g * gw + c * LANES:g * gw + (c + 1) * LANES]
                for r in range(d):
                    outs[g][r, :, c * LANES:(c + 1) * LANES] = buf[pl.ds(r, tm // d, stride=d), :].astype(_BF16)

    a = A_WIDTH
    c = B_WIDTH
    emit(rope(_dot(xb, w_ref[:, 0:a])), (q1_ref, q2_ref, q3_ref), 0)
    emit(rope(_dot(xb, w_ref[:, a:2 * a])), (k1_ref, k2_ref, k3_ref), 1)
    emit(_dot(xb, w_ref[:, 2 * a:3 * a]), (v1_ref, v2_ref, v3_ref), 2)
    qb_ref[...] = _dot(xb, w_ref[:, 3 * a:3 * a + c]).astype(_BF16)
    kb_ref[...] = _dot(xb, w_ref[:, 3 * a + c:3 * a + 2 * c]).astype(_BF16)
    vb_ref[...] = _dot(xb, w_ref[:, 3 * a + 2 * c:3 * a + 3 * c]).astype(_BF16)


def _rope_tables(seq):
    half = ROPE_DIMS // 2
    inv = ROPE_THETA ** (-jnp.arange(half, dtype=_F32) / half)
    ang = jnp.arange(seq, dtype=_F32)[:, None] * inv[None, :]
    cos = jnp.cos(ang)
    sin = jnp.sin(ang)
    ones = jnp.ones((seq, HEAD_DIM - ROPE_DIMS), _F32)
    zeros = jnp.zeros((seq, HEAD_DIM - ROPE_DIMS), _F32)
    zh = jnp.zeros((seq, half), _F32)
    cos_h = jnp.concatenate([cos, cos, ones], axis=1)
    sa_h = jnp.concatenate([-sin, zh, zeros], axis=1)
    sb_h = jnp.concatenate([zh, sin, zeros], axis=1)
    rep = LANES // HEAD_DIM
    return (jnp.tile(cos_h, (1, rep)), jnp.tile(sa_h, (1, rep)), jnp.tile(sb_h, (1, rep)))


def _project(x, w_qkv, tables, tm=1024):
    bsz, seq, d = x.shape
    cos, sa, sb = tables
    grid = (bsz, seq // tm)
    tok = lambda w: pl.BlockSpec((None, tm, w), lambda b, t: (b, t, 0))
    tab = pl.BlockSpec((tm, LANES), lambda b, t: (t, 0))
    gw = A_GROUP_WIDTH
    a_specs, a_shapes = [], []
    for dil in A_DILATIONS:
        if dil == 1:
            a_specs.append(tok(gw))
            a_shapes.append(jax.ShapeDtypeStruct((bsz, seq, gw), _BF16))
        else:
            a_specs.append(pl.BlockSpec((None, dil, tm // dil, gw), lambda b, t: (b, 0, t, 0)))
            a_shapes.append(jax.ShapeDtypeStruct((bsz, dil, seq // dil, gw), _BF16))
    b_shape = jax.ShapeDtypeStruct((bsz, seq, B_WIDTH), _BF16)
    out_specs = [a_specs[g] for g in range(3) for _ in range(3)] + [tok(B_WIDTH)] * 3
    out_shape = [a_shapes[g] for g in range(3) for _ in range(3)] + [b_shape] * 3
    return pl.pallas_call(
        _proj_kernel,
        grid=grid,
        in_specs=[tok(d), _resident(w_qkv.shape), tab, tab, tab],
        out_specs=out_specs,
        out_shape=out_shape,
        scratch_shapes=[pltpu.VMEM((6 * gw // LANES, tm, LANES), _F32)],
        compiler_params=pltpu.CompilerParams(
            dimension_semantics=("parallel", "parallel"), vmem_limit_bytes=VMEM_LIMIT),
        name="proj_rope",
    )(x, w_qkv, cos, sa, sb)


def _mixer_a_kernel(mask_ref, q1_ref, q2_ref, q3_ref, *rest, seq):
    ngroups = len(A_GROUPS)
    k_parts, v_parts = rest[:3 * ngroups], rest[3 * ngroups:6 * ngroups]
    y_ref = rest[6 * ngroups]
    o2_scr, l2_scr, o3_scr, l3_scr, s_scr, p_scr, inv_scr, l1_scr = rest[6 * ngroups + 1:6 * ngroups + 9]
    k_refs = rest[6 * ngroups + 9:7 * ngroups + 9]
    v_refs = rest[7 * ngroups + 9:]
    t = pl.program_id(1)
    qb = A_QBLOCK
    nh = A_HEADS_PER_GROUP
    nblk = A_TILE // qb
    sel = _head_select(qb, A_GROUP_WIDTH)
    left = lax.broadcasted_iota(jnp.int32, (qb, LANES), 1) < HEAD_DIM
    nhalf = A_GROUP_WIDTH // LANES
    q_refs = (q1_ref, q2_ref, q3_ref)

    for parts, bufs in ((k_parts, k_refs), (v_parts, v_refs)):
        for g in range(ngroups):
            cur, lo, hi = parts[3 * g:3 * g + 3]
            rows = cur.shape[1]
            bufs[g][:, 0:A_SIDE, :] = lo[...]
            bufs[g][:, A_SIDE:A_SIDE + rows, :] = cur[...]
            bufs[g][:, A_SIDE + rows:, :] = hi[...]

    def place(g, n):
        d = A_DILATIONS[g]
        per_res = nblk // d
        r, i = n // per_res, n % per_res
        sub_len = seq // d
        l0 = t * (A_TILE // d) + i * qb
        case = jnp.where(l0 == 0, 0, jnp.where(l0 == sub_len - qb, 2, 1))
        row0 = i * qb
        if not isinstance(row0, int):
            row0 = pl.multiple_of(row0, qb)
        return r, row0, row0, case

    def plane(refs, g, r):
        return refs[g].at[r]

    def scores(g, n, slot):
        r, row0, ks, _ = place(g, n)
        q = plane(q_refs, g, r)[pl.ds(row0, qb), :]
        qexp = jnp.where(sel, jnp.concatenate([q] * nh, axis=0), jnp.zeros((), _BF16))
        s_scr[slot] = _dot_nt(qexp, plane(k_refs, g, r)[pl.ds(ks, A_KWIN), :])

    def softmax(g, n, slot):
        r, row0, _, case = place(g, n)
        mask = mask_ref[case]
        maxes, dens = [], []
        s_blk = s_scr.at[slot]
        p_blk = p_scr.at[slot]
        for h in range(nh):
            rows = slice(h * qb, (h + 1) * qb)
            s = s_blk[rows, :] + mask
            m = jnp.max(s, axis=-1, keepdims=True)
            e = jnp.exp2(s - m)
            p_blk[rows, :] = e.astype(_BF16)
            maxes.append(m)
            dens.append(jnp.sum(e, axis=-1, keepdims=True))
        pair = lambda v, c: jnp.where(left, v[2 * c], v[2 * c + 1])
        den = [pair(dens, c) for c in range(nhalf)]
        inv_scr[slot] = jnp.concatenate([1.0 / dc for dc in den], axis=1)
        lse = jnp.concatenate([pair(maxes, c) + jnp.log2(den[c]) for c in range(nhalf)], axis=1)
        if g == 0:
            l1_scr[slot] = lse
        else:
            d = A_DILATIONS[g]
            scatter((l2_scr, l3_scr)[g - 1], g, row0, r, lse)

    def output(g, n, slot):
        r, row0, ks, _ = place(g, n)
        vw = plane(v_refs, g, r)[pl.ds(ks, A_KWIN), :]
        p_blk = p_scr.at[slot]
        halves = []
        for c in range(nhalf):
            pv = _dot(p_blk[2 * c * qb:2 * (c + 1) * qb, :], vw[:, c * LANES:(c + 1) * LANES])
            halves.append(jnp.where(left, pv[:qb], pv[qb:]))
        o = jnp.concatenate(halves, axis=1) * inv_scr[slot]
        if g != 0:
            d = A_DILATIONS[g]
            scatter((o2_scr, o3_scr)[g - 1], g, row0, r, o)
            return
        l1 = l1_scr[slot]
        l2 = gather(l2_scr, 1, row0)
        l3 = gather(l3_scr, 2, row0)
        lm = jnp.maximum(jnp.maximum(l1, l2), l3)
        e1 = jnp.exp2(l1 - lm)
        e2 = jnp.exp2(l2 - lm)
        e3 = jnp.exp2(l3 - lm)
        ya = (e1 * o + e2 * gather(o2_scr, 1, row0) + e3 * gather(o3_scr, 2, row0)) * (1.0 / (e1 + e2 + e3))
        y_ref[pl.ds(row0, qb), :] = ya.astype(_BF16)

    def scatter(scr, g, row0, r, val):
        pitch = A_SCRATCH_PITCH[g]
        for c in range(nhalf):
            scr[c, pl.ds(row0 * pitch + r, qb, stride=pitch), :] = val[:, c * LANES:(c + 1) * LANES]

    def gather(scr, g, tok0):
        d, pitch = A_DILATIONS[g], A_SCRATCH_PITCH[g]
        if pitch == d:
            starts, size = [tok0], qb
        else:
            starts = [pl.multiple_of((tok0 // d + l) * pitch, math.gcd(pitch, 8)) for l in range(qb // d)]
            size = d
        return jnp.concatenate(
            [jnp.concatenate([scr[c, pl.ds(st, size), :] for st in starts], axis=0) for c in range(nhalf)], axis=1)

    order = (2, 1, 0)
    wide = A_PIPE_WIDTH
    steps_per_group = nblk // wide
    total = steps_per_group * len(order)

    def step(k, g_of, local_of, parity):
        for w in range(wide):
            if 0 <= k < total:
                output(g_of(k), wide * local_of(k) + w, (parity, w))
        for w in range(wide):
            if k + 2 < total:
                scores(g_of(k + 2), wide * local_of(k + 2) + w, (parity, w))
        for w in range(wide):
            if 0 <= k + 1 < total:
                softmax(g_of(k + 1), wide * local_of(k + 1) + w, (1 - parity, w))

    static_g = lambda j: order[j // steps_per_group]
    static_local = lambda j: j % steps_per_group
    step(-2, static_g, static_local, 0)
    step(-1, static_g, static_local, 1)
    for gi, g in enumerate(order):
        base = gi * steps_per_group

        def steady(m, carry, g=g, base=base):
            for par in (0, 1):
                step(base + par, lambda j: g, lambda j: 2 * m + (j - base), par)
            return carry

        lax.fori_loop(0, (steps_per_group - 2) // 2, steady, 0)
        for k in (base + steps_per_group - 2, base + steps_per_group - 1):
            step(k, static_g, static_local, k % 2)


def _a_masks():
    qi = np.arange(A_QBLOCK)[:, None]
    kj = np.arange(A_KWIN)[None, :]
    band = np.abs(kj - A_SIDE - qi) <= A_SIDE
    cases = (band & (kj >= A_SIDE), band, band & (kj < A_KWIN - A_SIDE))
    return jnp.asarray(np.stack([np.where(ok, 0.0, NEG_INF) for ok in cases]), _F32)


def _mixer_a(qkv, masks):
    bsz, seq, gw = qkv[0].shape
    assert seq % A_TILE == 0 and seq // max(A_DILATIONS) >= A_KWIN
    q1, k1, v1, q2, k2, v2, q3, k3, v3 = [a.reshape(bsz, -1, a.shape[-2], gw) for a in qkv]

    def tile_spec(dil):
        return pl.BlockSpec((None, dil, A_TILE // dil, gw), lambda b, t: (b, 0, t, 0))

    def halo_specs(dil):
        per_tile = A_TILE // dil // A_SIDE
        last = seq // dil // A_SIDE - 1
        lo = pl.BlockSpec((None, dil, A_SIDE, gw), lambda b, t: (b, 0, jnp.maximum(t * per_tile - 1, 0), 0))
        hi = pl.BlockSpec((None, dil, A_SIDE, gw), lambda b, t: (b, 0, jnp.minimum((t + 1) * per_tile, last), 0))
        return [lo, hi]

    qs = [tile_spec(d) for d in A_DILATIONS]
    kvs = [s for d in A_DILATIONS for s in [tile_spec(d)] + halo_specs(d)]
    kv_bufs = [pltpu.VMEM((d, A_TILE // d + 2 * A_SIDE, gw), _BF16) for d in A_DILATIONS]
    return pl.pallas_call(
        functools.partial(_mixer_a_kernel, seq=seq),
        grid=(bsz, seq // A_TILE),
        in_specs=[_resident(masks.shape)] + qs + kvs + kvs,
        out_specs=pl.BlockSpec((None, A_TILE, gw), lambda b, t: (b, t, 0)),
        out_shape=jax.ShapeDtypeStruct((bsz, seq, gw), _BF16),
        scratch_shapes=[pltpu.VMEM((gw // LANES, A_TILE // A_DILATIONS[g] * A_SCRATCH_PITCH[g], LANES), _F32)
                        for g in (1, 1, 2, 2)] + [
            pltpu.VMEM((2, A_PIPE_WIDTH, A_HEADS_PER_GROUP * A_QBLOCK, A_KWIN), _F32),
            pltpu.VMEM((2, A_PIPE_WIDTH, A_HEADS_PER_GROUP * A_QBLOCK, A_KWIN), _BF16),
            pltpu.VMEM((2, A_PIPE_WIDTH, A_QBLOCK, gw), _F32),
            pltpu.VMEM((2, A_PIPE_WIDTH, A_QBLOCK, gw), _F32),
        ] + kv_bufs + kv_bufs,
        compiler_params=pltpu.CompilerParams(
            dimension_semantics=("parallel", "arbitrary"), vmem_limit_bytes=VMEM_LIMIT),
        name="mixer_a",
    )(masks, q1, q2, q3, *[a for a in (k1, k2, k3) for _ in range(3)], *[a for a in (v1, v2, v3) for _ in range(3)])


def _na_col_start(j):
    return min(max(j * NA_QCOLS - NA_COLS // 2, 0), GRID_W - NA_KCOLS)


def _na_bias_table(rpb, rows):
    nrb = rows // NA_QROWS
    qr = np.arange(NA_QROWS)
    qc = np.arange(NA_QCOLS)
    kr = np.arange(NA_KROWS)
    kc = np.arange(NA_KCOLS)
    dr_all, okr_all = [], []
    for i in (0, 1, nrb - 1):
        r_abs = i * NA_QROWS + qr
        win_r = np.clip(r_abs - NA_ROWS // 2, 0, rows - NA_ROWS)
        k_abs = min(max(i * NA_QROWS - NA_ROWS // 2, 0), rows - NA_KROWS) + kr
        okr_all.append((k_abs[None, :] >= win_r[:, None]) & (k_abs[None, :] < win_r[:, None] + NA_ROWS))
        dr_all.append(np.clip(k_abs[None, :] - r_abs[:, None] + NA_ROWS - 1, 0, 2 * NA_ROWS - 2))
    dc_all, okc_all = [], []
    for j in range(GRID_W // NA_QCOLS):
        c_abs = j * NA_QCOLS + qc
        win_c = np.clip(c_abs - NA_COLS // 2, 0, GRID_W - NA_COLS)
        k_abs = _na_col_start(j) + kc
        okc_all.append((k_abs[None, :] >= win_c[:, None]) & (k_abs[None, :] < win_c[:, None] + NA_COLS))
        dc_all.append(np.clip(k_abs[None, :] - c_abs[:, None] + NA_COLS - 1, 0, 2 * NA_COLS - 2))
    dr = np.stack(dr_all)
    okr = np.stack(okr_all)
    dc = np.stack(dc_all)
    okc = np.stack(okc_all)
    x_r = np.repeat(kr, NA_KCOLS)
    x_c = np.tile(kc, NA_KROWS)
    dr_x = dr[:, :, x_r]
    dc_x = dc[:, :, x_c]
    ok = okr[:, :, x_r][:, None, None, :, None, :] & okc[:, :, x_c][None, :, None, None, :, :]
    csel = (jnp.asarray(dc_x, jnp.int32)[..., None] == jnp.arange(2 * NA_COLS - 1)).astype(_F32)
    cols = jnp.einsum("hab,cqxb->achqx", rpb.astype(_F32), csel, precision=lax.Precision.HIGHEST)
    bias = jnp.zeros((), _F32)
    for a in range(2 * NA_ROWS - 1):
        pick = jnp.asarray(dr_x == a)[:, None, None, :, None, :]
        bias = bias + jnp.where(pick, cols[a][None, :, :, None, :, :], 0.0)
    bias = jnp.where(jnp.asarray(ok), bias * LOG2E, NEG_INF)
    return bias.reshape(3, GRID_W // NA_QCOLS, B_HEADS, NA_QROWS * NA_QCOLS, NA_KROWS * NA_KCOLS)


def _attn_b_kernel(bias_ref, q_ref, k0_ref, k1_ref, k2_ref, k3_ref,
                   v0_ref, v1_ref, v2_ref, v3_ref, o_ref, s_scr, p_scr, inv_scr):
    k_refs = (k0_ref, k1_ref, k2_ref, k3_ref)
    v_refs = (v0_ref, v1_ref, v2_ref, v3_ref)
    nq = NA_QROWS * NA_QCOLS
    nh = A_HEADS_PER_GROUP
    half_w = nh * HEAD_DIM
    nhalves = B_HEADS // nh
    sel = _head_select(nq, half_w)
    left = lax.broadcasted_iota(jnp.int32, (nq, LANES), 1) < HEAD_DIM

    def lanes_of(b):
        hh = b % nhalves
        return slice(hh * half_w, (hh + 1) * half_w)

    def window(refs, b):
        cs = _na_col_start(b // nhalves)
        return jnp.concatenate(
            [refs[kr // NA_PIECE_ROWS][(kr % NA_PIECE_ROWS) * GRID_W + cs:
                                       (kr % NA_PIECE_ROWS) * GRID_W + cs + NA_KCOLS, lanes_of(b)]
             for kr in range(NA_KROWS)], axis=0)

    def scores(b, slot):
        j = b // nhalves
        q = jnp.concatenate(
            [q_ref[r * GRID_W + j * NA_QCOLS:r * GRID_W + (j + 1) * NA_QCOLS, lanes_of(b)]
             for r in range(NA_QROWS)], axis=0)
        qexp = jnp.where(sel, jnp.concatenate([q] * nh, axis=0), jnp.zeros((), _BF16))
        s_scr[slot] = _dot_nt(qexp, window(k_refs, b))

    def softmax(b, slot):
        j, hh = b // nhalves, b % nhalves
        dens = []
        for h in range(nh):
            rows = slice(h * nq, (h + 1) * nq)
            s = s_scr[slot, rows, :] + bias_ref[j, hh * nh + h]
            e = jnp.exp2(s - jnp.max(s, axis=-1, keepdims=True))
            dens.append(jnp.sum(e, axis=-1, keepdims=True))
            p_scr[slot, rows, :] = e.astype(_BF16)
        inv_scr[slot] = jnp.concatenate(
            [1.0 / jnp.where(left, dens[2 * c], dens[2 * c + 1]) for c in range(half_w // LANES)], axis=1)

    def output(b, slot):
        j = b // nhalves
        pv = _dot(p_scr[slot], window(v_refs, b))
        ob = (_diag_blocks(pv, nq, left) * inv_scr[slot]).astype(_BF16)
        for r in range(NA_QROWS):
            o_ref[r * GRID_W + j * NA_QCOLS:r * GRID_W + (j + 1) * NA_QCOLS, lanes_of(b)] = (
                ob[r * NA_QCOLS:(r + 1) * NA_QCOLS, :])

    total = (GRID_W // NA_QCOLS) * nhalves
    for it in range(total + 2):
        if it < total:
            scores(it, it % 2)
        if 1 <= it <= total:
            softmax(it - 1, (it - 1) % 2)
        if it >= 2:
            output(it - 2, it % 2)


def _mixer_b(qb, kb, vb, bias_table):
    bsz, seq, _ = qb.shape
    rows = seq // GRID_W
    nrb = rows // NA_QROWS
    q_tokens = NA_QROWS * GRID_W
    piece_tokens = NA_PIECE_ROWS * GRID_W
    npieces = NA_KROWS // NA_PIECE_ROWS
    last_piece = rows // NA_PIECE_ROWS - npieces

    def piece0(i):
        return jnp.clip(i * (NA_QROWS // NA_PIECE_ROWS) - 1, 0, last_piece)

    def row_case(i):
        return jnp.where(i == 0, 0, jnp.where(i == nrb - 1, 2, 1))

    bias_spec = pl.BlockSpec((None,) + bias_table.shape[1:], lambda b, i: (row_case(i), 0, 0, 0, 0))
    q_spec = pl.BlockSpec((None, q_tokens, B_WIDTH), lambda b, i: (b, i, 0))
    piece_specs = [pl.BlockSpec((None, piece_tokens, B_WIDTH), lambda b, i, n=n: (b, piece0(i) + n, 0))
                   for n in range(npieces)]
    return pl.pallas_call(
        _attn_b_kernel,
        grid=(bsz, nrb),
        in_specs=[bias_spec, q_spec] + piece_specs + piece_specs,
        out_specs=q_spec,
        out_shape=jax.ShapeDtypeStruct((bsz, seq, B_WIDTH), _BF16),
        scratch_shapes=[
            pltpu.VMEM((2, A_HEADS_PER_GROUP * NA_QROWS * NA_QCOLS, NA_KROWS * NA_KCOLS), _F32),
            pltpu.VMEM((2, A_HEADS_PER_GROUP * NA_QROWS * NA_QCOLS, NA_KROWS * NA_KCOLS), _BF16),
            pltpu.VMEM((2, NA_QROWS * NA_QCOLS, A_GROUP_WIDTH), _F32),
        ],
        compiler_params=pltpu.CompilerParams(
            dimension_semantics=("parallel", "arbitrary"), vmem_limit_bytes=VMEM_LIMIT),
        name="mixer_b",
    )(bias_table, qb, kb, kb, kb, kb, vb, vb, vb, vb)


def _layer_norm(h, g, b):
    mu = jnp.mean(h, axis=-1, keepdims=True)
    c = h - mu
    var = jnp.mean(c * c, axis=-1, keepdims=True)
    return c * lax.rsqrt(var + LN_EPS) * g + b


def _post_kernel(x_ref, ya_ref, yb_ref,
                 wg_ref, bg_ref, wa_ref, wb_ref, wo_ref, ln1g_ref, ln1b_ref,
                 w1_ref, b1_ref, w2_ref, b2_ref, ln2g_ref, ln2b_ref, y_ref, *, alpha):
    x = x_ref[...]
    xb = x.astype(_BF16)
    ya = _dot(ya_ref[...], wa_ref[...])
    yb = _dot(yb_ref[...], wb_ref[...])
    ga = jax.nn.sigmoid(_dot(xb, wg_ref[:, :D_MODEL]) + bg_ref[0:1, :])
    gb = jax.nn.sigmoid(_dot(xb, wg_ref[:, D_MODEL:]) + bg_ref[1:2, :])
    merged = ga * ya + gb * yb
    h = alpha * x + _dot(merged.astype(_BF16), wo_ref[...])
    x1 = _layer_norm(h, ln1g_ref[...], ln1b_ref[...])
    x1b = x1.astype(_BF16)
    h2 = alpha * x1 + b2_ref[...]
    for c in range(D_FF // FF_CHUNK):
        cols = slice(c * FF_CHUNK, (c + 1) * FF_CHUNK)
        hid = jnp.maximum(_dot(x1b, w1_ref[:, cols]) + b1_ref[:, cols], 0.0)
        h2 = h2 + _dot((hid * hid).astype(_BF16), w2_ref[cols, :])
    y_ref[...] = _layer_norm(h2, ln2g_ref[...], ln2b_ref[...])


def _post(x, ya, yb, p, alpha, tm=512):
    bsz, seq, d = x.shape
    tok = lambda w: pl.BlockSpec((None, tm, w), lambda b, t: (b, t, 0))
    weights = (p["w_gate"], p["b_gate"], p["w_a"], p["w_b"], p["w_o"], p["ln1_g"], p["ln1_b"],
               p["w_ff1"], p["b_ff1"], p["w_ff2"], p["b_ff2"], p["ln2_g"], p["ln2_b"])
    return pl.pallas_call(
        functools.partial(_post_kernel, alpha=alpha),
        grid=(bsz, seq // tm),
        in_specs=[tok(d), tok(A_GROUP_WIDTH), tok(B_WIDTH)] + [_resident(w.shape) for w in weights],
        out_specs=tok(d),
        out_shape=jax.ShapeDtypeStruct((bsz, seq, d), _F32),
        compiler_params=pltpu.CompilerParams(
            dimension_semantics=("parallel", "parallel"), vmem_limit_bytes=VMEM_LIMIT),
        name="post_ffn",
    )(x, ya, yb, *weights)


def _encoder_layer(x, p, consts, alpha):
    outs = _project(x, p["w_qkv"], consts["rope"])
    ya = _mixer_a(outs[:9], consts["a_masks"])
    yb = _mixer_b(*outs[9:], p["na_bias"])
    return _post(x, ya, yb, p, alpha)


def kernel(x_prompt, x_sample, w_in, b_gate, w_branch_a, w_branch_b, w_out, rel_pos_bias,
           ln1_g, ln1_b, w_ff1, b_ff1, w_ff2, b_ff2, ln2_g, ln2_b):
    depth = w_in.shape[0]
    alpha = (2.0 * depth) ** 0.25
    seq = x_prompt.shape[1]
    assert x_sample.shape[1] == seq and seq % (GRID_W * NA_KROWS) == 0
    consts = {"rope": _rope_tables(seq), "a_masks": _a_masks()}
    scale = HEAD_DIM ** -0.5 * LOG2E
    col_scale = np.ones((QKV_COLS,), np.float32)
    col_scale[0:A_WIDTH] = scale
    col_scale[3 * A_WIDTH:3 * A_WIDTH + B_WIDTH] = scale
    row = lambda v: v.reshape(1, -1)
    y_prompt, y_sample = x_prompt, x_sample
    for layer in range(depth):
        p = {
            "w_qkv": (w_in[layer][:, :QKV_COLS] * col_scale).astype(_BF16),
            "w_gate": w_in[layer][:, QKV_COLS:].astype(_BF16),
            "b_gate": b_gate[layer],
            "w_a": w_branch_a[layer].astype(_BF16),
            "w_b": w_branch_b[layer].astype(_BF16),
            "w_o": w_out[layer].astype(_BF16),
            "na_bias": _na_bias_table(rel_pos_bias[layer], seq // GRID_W),
            "ln1_g": row(ln1_g[layer]), "ln1_b": row(ln1_b[layer]),
            "w_ff1": w_ff1[layer].astype(_BF16), "b_ff1": row(b_ff1[layer]),
            "w_ff2": w_ff2[layer].astype(_BF16), "b_ff2": row(b_ff2[layer]),
            "ln2_g": row(ln2_g[layer]), "ln2_b": row(ln2_b[layer]),
        }
        y_prompt = _encoder_layer(y_prompt, p, consts, alpha)
        y_sample = _encoder_layer(y_sample, p, consts, alpha)
    return (y_prompt, y_sample)
```

```python
import functools
import math

import numpy as np
import jax
import jax.numpy as jnp
from jax import lax
from jax.experimental import pallas as pl
from jax.experimental.pallas import tpu as pltpu

D_MODEL = 1024
HEAD_DIM = 64
A_GROUPS = ((128, 1), (512, 4), (2048, 16))
A_DILATIONS = tuple(d for _, d in A_GROUPS)
A_HEADS_PER_GROUP = 4
A_GROUP_WIDTH = A_HEADS_PER_GROUP * HEAD_DIM
A_WIDTH = len(A_GROUPS) * A_GROUP_WIDTH
A_SIDE = 64
A_QBLOCK = 128
A_KWIN = A_QBLOCK + 2 * A_SIDE
A_TILE = A_QBLOCK * max(A_DILATIONS)
A_PIPE_WIDTH = 2
A_SCRATCH_PITCH = (1, 4, 20)
B_HEADS = 8
B_WIDTH = B_HEADS * HEAD_DIM
GRID_W = 64
NA_ROWS = 8
NA_COLS = 16
NA_QROWS = 8
NA_QCOLS = 16
NA_KROWS = 16
NA_KCOLS = 32
NA_PIECE_ROWS = 4
ROPE_THETA = 500000.0
ROPE_DIMS = HEAD_DIM // 4
D_FF = 4 * D_MODEL
FF_CHUNK = 1024
QKV_COLS = 3 * A_WIDTH + 3 * B_WIDTH
LN_EPS = 1e-5
NEG_INF = -1e30
LOG2E = math.log2(math.e)
LANES = 128

VMEM_LIMIT = 56 * 1024 * 1024

_BF16 = jnp.bfloat16
_F32 = jnp.float32


def _dot(a, b):
    return jnp.dot(a, b, preferred_element_type=_F32)


def _dot_nt(a, b):
    return lax.dot_general(a, b, (((1,), (1,)), ((), ())), preferred_element_type=_F32)


def _resident(shape):
    nd = len(shape)
    return pl.BlockSpec(shape, lambda *_: (0,) * nd, pipeline_mode=pl.Buffered(1))


def _head_select(rows_per_head, width):
    n = A_HEADS_PER_GROUP * rows_per_head
    rh = lax.broadcasted_iota(jnp.int32, (n, width), 0) // rows_per_head
    lh = lax.broadcasted_iota(jnp.int32, (n, width), 1) // HEAD_DIM
    return rh == lh


def _diag_blocks(t, nq, left):
    cols = (lambda a, c: a) if t.shape[1] == 1 else (lambda a, c: a[:, c * LANES:(c + 1) * LANES])
    halves = [jnp.where(left, cols(t[(2 * c) * nq:(2 * c + 1) * nq], c), cols(t[(2 * c + 1) * nq:(2 * c + 2) * nq], c))
              for c in range(A_GROUP_WIDTH // LANES)]
    return jnp.concatenate(halves, axis=1)


def _proj_kernel(x_ref, w_ref, cos_ref, sa_ref, sb_ref,
                 q1_ref, k1_ref, v1_ref, q2_ref, k2_ref, v2_ref, q3_ref, k3_ref, v3_ref,
                 qb_ref, kb_ref, vb_ref, scr_ref, mid_ref):
    tm = x_ref.shape[0]
    xb = x_ref[...].astype(_BF16)
    cos = cos_ref[...]
    sa = sa_ref[...]
    sb = sb_ref[...]

    def rope(acc):
        outs = []
        for c in range(A_WIDTH // LANES):
            ch = acc[:, c * LANES:(c + 1) * LANES]
            up = pltpu.roll(ch, LANES - ROPE_DIMS // 2, 1)
            dn = pltpu.roll(ch, ROPE_DIMS // 2, 1)
            outs.append(ch * cos + up * sa + dn * sb)
        return jnp.concatenate(outs, axis=1)

    def emit(acc, outs, slot):
        gw = A_GROUP_WIDTH
        outs[0][...] = acc[:, 0:gw].astype(_BF16)
        for g in (1, 2):
            d = A_DILATIONS[g]
            for c in range(gw // LANES):
                buf = scr_ref.at[(2 * slot + g - 1) * (gw // LANES) + c]
                buf[...] = acc[:, g * gw + c * LANES:g * gw + (c + 1) * LANES]
                lanes = slice(c * LANES, (c + 1) * LANES)
                if d == 4:
                    for r in range(d):
                        outs[g][r, :, lanes] = buf[pl.ds(r, tm // d, stride=d), :].astype(_BF16)
                else:
                    mid = mid_ref.at[slot * (gw // LANES) + c]
                    for lo in range(4):
                        mid[lo] = buf[pl.ds(lo, tm // 4, stride=4), :]
                    for hi in range(d // 4):
                        for lo in range(4):
                            outs[g][4 * hi + lo, :, lanes] = (
                                mid.at[lo][pl.ds(hi, tm // d, stride=4), :].astype(_BF16))

    a = A_WIDTH
    c = B_WIDTH
    emit(rope(_dot(xb, w_ref[:, 0:a])), (q1_ref, q2_ref, q3_ref), 0)
    emit(rope(_dot(xb, w_ref[:, a:2 * a])), (k1_ref, k2_ref, k3_ref), 1)
    emit(_dot(xb, w_ref[:, 2 * a:3 * a]), (v1_ref, v2_ref, v3_ref), 2)
    qb_ref[...] = _dot(xb, w_ref[:, 3 * a:3 * a + c]).astype(_BF16)
    kb_ref[...] = _dot(xb, w_ref[:, 3 * a + c:3 * a + 2 * c]).astype(_BF16)
    vb_ref[...] = _dot(xb, w_ref[:, 3 * a + 2 * c:3 * a + 3 * c]).astype(_BF16)


def _rope_tables(seq):
    half = ROPE_DIMS // 2
    inv = ROPE_THETA ** (-jnp.arange(half, dtype=_F32) / half)
    ang = jnp.arange(seq, dtype=_F32)[:, None] * inv[None, :]
    cos = jnp.cos(ang)
    sin = jnp.sin(ang)
    ones = jnp.ones((seq, HEAD_DIM - ROPE_DIMS), _F32)
    zeros = jnp.zeros((seq, HEAD_DIM - ROPE_DIMS), _F32)
    zh = jnp.zeros((seq, half), _F32)
    cos_h = jnp.concatenate([cos, cos, ones], axis=1)
    sa_h = jnp.concatenate([-sin, zh, zeros], axis=1)
    sb_h = jnp.concatenate([zh, sin, zeros], axis=1)
    rep = LANES // HEAD_DIM
    return (jnp.tile(cos_h, (1, rep)), jnp.tile(sa_h, (1, rep)), jnp.tile(sb_h, (1, rep)))


def _project(x, w_qkv, tables, tm=1024):
    bsz, seq, d = x.shape
    cos, sa, sb = tables
    grid = (bsz, seq // tm)
    tok = lambda w: pl.BlockSpec((None, tm, w), lambda b, t: (b, t, 0))
    tab = pl.BlockSpec((tm, LANES), lambda b, t: (t, 0))
    gw = A_GROUP_WIDTH
    a_specs, a_shapes = [], []
    for dil in A_DILATIONS:
        if dil == 1:
            a_specs.append(tok(gw))
            a_shapes.append(jax.ShapeDtypeStruct((bsz, seq, gw), _BF16))
        else:
            a_specs.append(pl.BlockSpec((None, dil, tm // dil, gw), lambda b, t: (b, 0, t, 0)))
            a_shapes.append(jax.ShapeDtypeStruct((bsz, dil, seq // dil, gw), _BF16))
    b_shape = jax.ShapeDtypeStruct((bsz, seq, B_WIDTH), _BF16)
    out_specs = [a_specs[g] for g in range(3) for _ in range(3)] + [tok(B_WIDTH)] * 3
    out_shape = [a_shapes[g] for g in range(3) for _ in range(3)] + [b_shape] * 3
    return pl.pallas_call(
        _proj_kernel,
        grid=grid,
        in_specs=[tok(d), _resident(w_qkv.shape), tab, tab, tab],
        out_specs=out_specs,
        out_shape=out_shape,
        scratch_shapes=[pltpu.VMEM((6 * gw // LANES, tm, LANES), _F32),
                        pltpu.VMEM((3 * gw // LANES, 4, tm // 4, LANES), _F32)],
        compiler_params=pltpu.CompilerParams(
            dimension_semantics=("parallel", "parallel"), vmem_limit_bytes=VMEM_LIMIT),
        name="proj_rope",
    )(x, w_qkv, cos, sa, sb)


def _mixer_a_kernel(mask_ref, q1_ref, q2_ref, q3_ref, *rest, seq):
    ngroups = len(A_GROUPS)
    k_parts, v_parts = rest[:3 * ngroups], rest[3 * ngroups:6 * ngroups]
    y_ref = rest[6 * ngroups]
    o2_scr, l2_scr, o3_scr, l3_scr, s_scr, p_scr, inv_scr, l1_scr = rest[6 * ngroups + 1:6 * ngroups + 9]
    k_refs = rest[6 * ngroups + 9:7 * ngroups + 9]
    v_refs = rest[7 * ngroups + 9:]
    t = pl.program_id(1)
    qb = A_QBLOCK
    nh = A_HEADS_PER_GROUP
    nblk = A_TILE // qb
    sel = _head_select(qb, A_GROUP_WIDTH)
    left = lax.broadcasted_iota(jnp.int32, (qb, LANES), 1) < HEAD_DIM
    nhalf = A_GROUP_WIDTH // LANES
    q_refs = (q1_ref, q2_ref, q3_ref)

    for parts, bufs in ((k_parts, k_refs), (v_parts, v_refs)):
        for g in range(ngroups):
            cur, lo, hi = parts[3 * g:3 * g + 3]
            rows = cur.shape[1]
            bufs[g][:, 0:A_SIDE, :] = lo[...]
            bufs[g][:, A_SIDE:A_SIDE + rows, :] = cur[...]
            bufs[g][:, A_SIDE + rows:, :] = hi[...]

    def place(g, n):
        d = A_DILATIONS[g]
        per_res = nblk // d
        r, i = n // per_res, n % per_res
        sub_len = seq // d
        l0 = t * (A_TILE // d) + i * qb
        case = jnp.where(l0 == 0, 0, jnp.where(l0 == sub_len - qb, 2, 1))
        row0 = i * qb
        if not isinstance(row0, int):
            row0 = pl.multiple_of(row0, qb)
        return r, row0, row0, case

    def plane(refs, g, r):
        return refs[g].at[r]

    def scores(g, n, slot):
        r, row0, ks, _ = place(g, n)
        q = plane(q_refs, g, r)[pl.ds(row0, qb), :]
        qexp = jnp.where(sel, jnp.concatenate([q] * nh, axis=0), jnp.zeros((), _BF16))
        s_scr[slot] = _dot_nt(qexp, plane(k_refs, g, r)[pl.ds(ks, A_KWIN), :])

    def softmax(g, n, slot):
        r, row0, _, case = place(g, n)
        mask = mask_ref[case]
        maxes, dens = [], []
        s_blk = s_scr.at[slot]
        p_blk = p_scr.at[slot]
        for h in range(nh):
            rows = slice(h * qb, (h + 1) * qb)
            s = s_blk[rows, :] + mask
            m = jnp.max(s, axis=-1, keepdims=True)
            e = jnp.exp2(s - m)
            p_blk[rows, :] = e.astype(_BF16)
            maxes.append(m)
            dens.append(jnp.sum(e, axis=-1, keepdims=True))
        pair = lambda v, c: jnp.where(left, v[2 * c], v[2 * c + 1])
        den = [pair(dens, c) for c in range(nhalf)]
        inv_scr[slot] = jnp.concatenate([1.0 / dc for dc in den], axis=1)
        lse = jnp.concatenate([pair(maxes, c) + jnp.log2(den[c]) for c in range(nhalf)], axis=1)
        if g == 0:
            l1_scr[slot] = lse
        else:
            d = A_DILATIONS[g]
            scatter((l2_scr, l3_scr)[g - 1], g, row0, r, lse)

    def output(g, n, slot):
        r, row0, ks, _ = place(g, n)
        vw = plane(v_refs, g, r)[pl.ds(ks, A_KWIN), :]
        p_blk = p_scr.at[slot]
        halves = []
        for c in range(nhalf):
            pv = _dot(p_blk[2 * c * qb:2 * (c + 1) * qb, :], vw[:, c * LANES:(c + 1) * LANES])
            halves.append(jnp.where(left, pv[:qb], pv[qb:]))
        o = jnp.concatenate(halves, axis=1) * inv_scr[slot]
        if g != 0:
            d = A_DILATIONS[g]
            scatter((o2_scr, o3_scr)[g - 1], g, row0, r, o)
            return
        l1 = l1_scr[slot]
        l2 = gather(l2_scr, 1, row0)
        l3 = gather(l3_scr, 2, row0)
        lm = jnp.maximum(jnp.maximum(l1, l2), l3)
        e1 = jnp.exp2(l1 - lm)
        e2 = jnp.exp2(l2 - lm)
        e3 = jnp.exp2(l3 - lm)
        ya = (e1 * o + e2 * gather(o2_scr, 1, row0) + e3 * gather(o3_scr, 2, row0)) * (1.0 / (e1 + e2 + e3))
        y_ref[pl.ds(row0, qb), :] = ya.astype(_BF16)

    def scatter(scr, g, row0, r, val):
        pitch = A_SCRATCH_PITCH[g]
        for c in range(nhalf):
            scr[c, pl.ds(row0 * pitch + r, qb, stride=pitch), :] = val[:, c * LANES:(c + 1) * LANES]

    def gather(scr, g, tok0):
        d, pitch = A_DILATIONS[g], A_SCRATCH_PITCH[g]
        if pitch == d:
            starts, size = [tok0], qb
        else:
            starts = [pl.multiple_of((tok0 // d + l) * pitch, math.gcd(pitch, 8)) for l in range(qb // d)]
            size = d
        return jnp.concatenate(
            [jnp.concatenate([scr[c, pl.ds(st, size), :] for st in starts], axis=0) for c in range(nhalf)], axis=1)

    order = (2, 1, 0)
    wide = A_PIPE_WIDTH
    steps_per_group = nblk // wide
    total = steps_per_group * len(order)

    def step(k, g_of, local_of, parity):
        for w in range(wide):
            if 0 <= k < total:
                output(g_of(k), wide * local_of(k) + w, (parity, w))
        for w in range(wide):
            if k + 2 < total:
                scores(g_of(k + 2), wide * local_of(k + 2) + w, (parity, w))
        for w in range(wide):
            if 0 <= k + 1 < total:
                softmax(g_of(k + 1), wide * local_of(k + 1) + w, (1 - parity, w))

    static_g = lambda j: order[j // steps_per_group]
    static_local = lambda j: j % steps_per_group
    step(-2, static_g, static_local, 0)
    step(-1, static_g, static_local, 1)
    for gi, g in enumerate(order):
        base = gi * steps_per_group

        def steady(m, carry, g=g, base=base):
            for par in (0, 1):
                step(base + par, lambda j: g, lambda j: 2 * m + (j - base), par)
            return carry

        lax.fori_loop(0, (steps_per_group - 2) // 2, steady, 0)
        for k in (base + steps_per_group - 2, base + steps_per_group - 1):
            step(k, static_g, static_local, k % 2)


def _a_masks():
    qi = np.arange(A_QBLOCK)[:, None]
    kj = np.arange(A_KWIN)[None, :]
    band = np.abs(kj - A_SIDE - qi) <= A_SIDE
    cases = (band & (kj >= A_SIDE), band, band & (kj < A_KWIN - A_SIDE))
    return jnp.asarray(np.stack([np.where(ok, 0.0, NEG_INF) for ok in cases]), _F32)


def _mixer_a(qkv, masks):
    bsz, seq, gw = qkv[0].shape
    assert seq % A_TILE == 0 and seq // max(A_DILATIONS) >= A_KWIN
    q1, k1, v1, q2, k2, v2, q3, k3, v3 = [a.reshape(bsz, -1, a.shape[-2], gw) for a in qkv]

    def tile_spec(dil):
        return pl.BlockSpec((None, dil, A_TILE // dil, gw), lambda b, t: (b, 0, t, 0))

    def halo_specs(dil):
        per_tile = A_TILE // dil // A_SIDE
        last = seq // dil // A_SIDE - 1
        lo = pl.BlockSpec((None, dil, A_SIDE, gw), lambda b, t: (b, 0, jnp.maximum(t * per_tile - 1, 0), 0))
        hi = pl.BlockSpec((None, dil, A_SIDE, gw), lambda b, t: (b, 0, jnp.minimum((t + 1) * per_tile, last), 0))
        return [lo, hi]

    qs = [tile_spec(d) for d in A_DILATIONS]
    kvs = [s for d in A_DILATIONS for s in [tile_spec(d)] + halo_specs(d)]
    kv_bufs = [pltpu.VMEM((d, A_TILE // d + 2 * A_SIDE, gw), _BF16) for d in A_DILATIONS]
    return pl.pallas_call(
        functools.partial(_mixer_a_kernel, seq=seq),
        grid=(bsz, seq // A_TILE),
        in_specs=[_resident(masks.shape)] + qs + kvs + kvs,
        out_specs=pl.BlockSpec((None, A_TILE, gw), lambda b, t: (b, t, 0)),
        out_shape=jax.ShapeDtypeStruct((bsz, seq, gw), _BF16),
        scratch_shapes=[pltpu.VMEM((gw // LANES, A_TILE // A_DILATIONS[g] * A_SCRATCH_PITCH[g], LANES), _F32)
                        for g in (1, 1, 2, 2)] + [
            pltpu.VMEM((2, A_PIPE_WIDTH, A_HEADS_PER_GROUP * A_QBLOCK, A_KWIN), _F32),
            pltpu.VMEM((2, A_PIPE_WIDTH, A_HEADS_PER_GROUP * A_QBLOCK, A_KWIN), _BF16),
            pltpu.VMEM((2, A_PIPE_WIDTH, A_QBLOCK, gw), _F32),
            pltpu.VMEM((2, A_PIPE_WIDTH, A_QBLOCK, gw), _F32),
        ] + kv_bufs + kv_bufs,
        compiler_params=pltpu.CompilerParams(
            dimension_semantics=("parallel", "arbitrary"), vmem_limit_bytes=VMEM_LIMIT),
        name="mixer_a",
    )(masks, q1, q2, q3, *[a for a in (k1, k2, k3) for _ in range(3)], *[a for a in (v1, v2, v3) for _ in range(3)])


def _na_col_start(j):
    return min(max(j * NA_QCOLS - NA_COLS // 2, 0), GRID_W - NA_KCOLS)


def _na_bias_table(rpb, rows):
    nrb = rows // NA_QROWS
    qr = np.arange(NA_QROWS)
    qc = np.arange(NA_QCOLS)
    kr = np.arange(NA_KROWS)
    kc = np.arange(NA_KCOLS)
    dr_all, okr_all = [], []
    for i in (0, 1, nrb - 1):
        r_abs = i * NA_QROWS + qr
        win_r = np.clip(r_abs - NA_ROWS // 2, 0, rows - NA_ROWS)
        k_abs = min(max(i * NA_QROWS - NA_ROWS // 2, 0), rows - NA_KROWS) + kr
        okr_all.append((k_abs[None, :] >= win_r[:, None]) & (k_abs[None, :] < win_r[:, None] + NA_ROWS))
        dr_all.append(np.clip(k_abs[None, :] - r_abs[:, None] + NA_ROWS - 1, 0, 2 * NA_ROWS - 2))
    dc_all, okc_all = [], []
    for j in range(GRID_W // NA_QCOLS):
        c_abs = j * NA_QCOLS + qc
        win_c = np.clip(c_abs - NA_COLS // 2, 0, GRID_W - NA_COLS)
        k_abs = _na_col_start(j) + kc
        okc_all.append((k_abs[None, :] >= win_c[:, None]) & (k_abs[None, :] < win_c[:, None] + NA_COLS))
        dc_all.append(np.clip(k_abs[None, :] - c_abs[:, None] + NA_COLS - 1, 0, 2 * NA_COLS - 2))
    dr = np.stack(dr_all)
    okr = np.stack(okr_all)
    dc = np.stack(dc_all)
    okc = np.stack(okc_all)
    x_r = np.repeat(kr, NA_KCOLS)
    x_c = np.tile(kc, NA_KROWS)
    dr_x = dr[:, :, x_r]
    dc_x = dc[:, :, x_c]
    ok = okr[:, :, x_r][:, None, None, :, None, :] & okc[:, :, x_c][None, :, None, None, :, :]
    csel = (jnp.asarray(dc_x, jnp.int32)[..., None] == jnp.arange(2 * NA_COLS - 1)).astype(_F32)
    cols = jnp.einsum("hab,cqxb->achqx", rpb.astype(_F32), csel, precision=lax.Precision.HIGHEST)
    bias = jnp.zeros((), _F32)
    for a in range(2 * NA_ROWS - 1):
        pick = jnp.asarray(dr_x == a)[:, None, None, :, None, :]
        bias = bias + jnp.where(pick, cols[a][None, :, :, None, :, :], 0.0)
    bias = jnp.where(jnp.asarray(ok), bias * LOG2E, NEG_INF)
    return bias.reshape(3, GRID_W // NA_QCOLS, B_HEADS, NA_QROWS * NA_QCOLS, NA_KROWS * NA_KCOLS)


def _attn_b_kernel(bias_ref, q_ref, k0_ref, k1_ref, k2_ref, k3_ref,
                   v0_ref, v1_ref, v2_ref, v3_ref, o_ref, s_scr, p_scr, inv_scr):
    k_refs = (k0_ref, k1_ref, k2_ref, k3_ref)
    v_refs = (v0_ref, v1_ref, v2_ref, v3_ref)
    nq = NA_QROWS * NA_QCOLS
    nh = A_HEADS_PER_GROUP
    half_w = nh * HEAD_DIM
    nhalves = B_HEADS // nh
    sel = _head_select(nq, half_w)
    left = lax.broadcasted_iota(jnp.int32, (nq, LANES), 1) < HEAD_DIM

    def lanes_of(b):
        hh = b % nhalves
        return slice(hh * half_w, (hh + 1) * half_w)

    def window(refs, b):
        cs = _na_col_start(b // nhalves)
        return jnp.concatenate(
            [refs[kr // NA_PIECE_ROWS][(kr % NA_PIECE_ROWS) * GRID_W + cs:
                                       (kr % NA_PIECE_ROWS) * GRID_W + cs + NA_KCOLS, lanes_of(b)]
             for kr in range(NA_KROWS)], axis=0)

    def scores(b, slot):
        j = b // nhalves
        q = jnp.concatenate(
            [q_ref[r * GRID_W + j * NA_QCOLS:r * GRID_W + (j + 1) * NA_QCOLS, lanes_of(b)]
             for r in range(NA_QROWS)], axis=0)
        qexp = jnp.where(sel, jnp.concatenate([q] * nh, axis=0), jnp.zeros((), _BF16))
        s_scr[slot] = _dot_nt(qexp, window(k_refs, b))

    def softmax(b, slot):
        j, hh = b // nhalves, b % nhalves
        dens = []
        for h in range(nh):
            rows = slice(h * nq, (h + 1) * nq)
            s = s_scr[slot, rows, :] + bias_ref[j, hh * nh + h]
            e = jnp.exp2(s - jnp.max(s, axis=-1, keepdims=True))
            dens.append(jnp.sum(e, axis=-1, keepdims=True))
            p_scr[slot, rows, :] = e.astype(_BF16)
        inv_scr[slot] = jnp.concatenate(
            [1.0 / jnp.where(left, dens[2 * c], dens[2 * c + 1]) for c in range(half_w // LANES)], axis=1)

    def output(b, slot):
        j = b // nhalves
        pv = _dot(p_scr[slot], window(v_refs, b))
        ob = (_diag_blocks(pv, nq, left) * inv_scr[slot]).astype(_BF16)
        for r in range(NA_QROWS):
            o_ref[r * GRID_W + j * NA_QCOLS:r * GRID_W + (j + 1) * NA_QCOLS, lanes_of(b)] = (
                ob[r * NA_QCOLS:(r + 1) * NA_QCOLS, :])

    total = (GRID_W // NA_QCOLS) * nhalves
    for it in range(total + 2):
        if it < total:
            scores(it, it % 2)
        if 1 <= it <= total:
            softmax(it - 1, (it - 1) % 2)
        if it >= 2:
            output(it - 2, it % 2)


def _mixer_b(qb, kb, vb, bias_table):
    bsz, seq, _ = qb.shape
    rows = seq // GRID_W
    nrb = rows // NA_QROWS
    q_tokens = NA_QROWS * GRID_W
    piece_tokens = NA_PIECE_ROWS * GRID_W
    npieces = NA_KROWS // NA_PIECE_ROWS
    last_piece = rows // NA_PIECE_ROWS - npieces

    def piece0(i):
        return jnp.clip(i * (NA_QROWS // NA_PIECE_ROWS) - 1, 0, last_piece)

    def row_case(i):
        return jnp.where(i == 0, 0, jnp.where(i == nrb - 1, 2, 1))

    bias_spec = pl.BlockSpec((None,) + bias_table.shape[1:], lambda b, i: (row_case(i), 0, 0, 0, 0))
    q_spec = pl.BlockSpec((None, q_tokens, B_WIDTH), lambda b, i: (b, i, 0))
    piece_specs = [pl.BlockSpec((None, piece_tokens, B_WIDTH), lambda b, i, n=n: (b, piece0(i) + n, 0))
                   for n in range(npieces)]
    return pl.pallas_call(
        _attn_b_kernel,
        grid=(bsz, nrb),
        in_specs=[bias_spec, q_spec] + piece_specs + piece_specs,
        out_specs=q_spec,
        out_shape=jax.ShapeDtypeStruct((bsz, seq, B_WIDTH), _BF16),
        scratch_shapes=[
            pltpu.VMEM((2, A_HEADS_PER_GROUP * NA_QROWS * NA_QCOLS, NA_KROWS * NA_KCOLS), _F32),
            pltpu.VMEM((2, A_HEADS_PER_GROUP * NA_QROWS * NA_QCOLS, NA_KROWS * NA_KCOLS), _BF16),
            pltpu.VMEM((2, NA_QROWS * NA_QCOLS, A_GROUP_WIDTH), _F32),
        ],
        compiler_params=pltpu.CompilerParams(
            dimension_semantics=("parallel", "arbitrary"), vmem_limit_bytes=VMEM_LIMIT),
        name="mixer_b",
    )(bias_table, qb, kb, kb, kb, kb, vb, vb, vb, vb)


def _layer_norm(h, g, b):
    mu = jnp.mean(h, axis=-1, keepdims=True)
    c = h - mu
    var = jnp.mean(c * c, axis=-1, keepdims=True)
    return c * lax.rsqrt(var + LN_EPS) * g + b


def _post_kernel(x_ref, ya_ref, yb_ref,
                 wg_ref, bg_ref, wa_ref, wb_ref, wo_ref, ln1g_ref, ln1b_ref,
                 w1_ref, b1_ref, w2_ref, b2_ref, ln2g_ref, ln2b_ref, y_ref, *, alpha):
    x = x_ref[...]
    xb = x.astype(_BF16)
    ya = _dot(ya_ref[...], wa_ref[...])
    yb = _dot(yb_ref[...], wb_ref[...])
    ga = jax.nn.sigmoid(_dot(xb, wg_ref[:, :D_MODEL]) + bg_ref[0:1, :])
    gb = jax.nn.sigmoid(_dot(xb, wg_ref[:, D_MODEL:]) + bg_ref[1:2, :])
    merged = ga * ya + gb * yb
    h = alpha * x + _dot(merged.astype(_BF16), wo_ref[...])
    x1 = _layer_norm(h, ln1g_ref[...], ln1b_ref[...])
    x1b = x1.astype(_BF16)
    h2 = alpha * x1 + b2_ref[...]
    for c in range(D_FF // FF_CHUNK):
        cols = slice(c * FF_CHUNK, (c + 1) * FF_CHUNK)
        hid = jnp.maximum(_dot(x1b, w1_ref[:, cols]) + b1_ref[:, cols], 0.0)
        h2 = h2 + _dot((hid * hid).astype(_BF16), w2_ref[cols, :])
    y_ref[...] = _layer_norm(h2, ln2g_ref[...], ln2b_ref[...])


def _post(x, ya, yb, p, alpha, tm=512):
    bsz, seq, d = x.shape
    tok = lambda w: pl.BlockSpec((None, tm, w), lambda b, t: (b, t, 0))
    weights = (p["w_gate"], p["b_gate"], p["w_a"], p["w_b"], p["w_o"], p["ln1_g"], p["ln1_b"],
               p["w_ff1"], p["b_ff1"], p["w_ff2"], p["b_ff2"], p["ln2_g"], p["ln2_b"])
    return pl.pallas_call(
        functools.partial(_post_kernel, alpha=alpha),
        grid=(bsz, seq // tm),
        in_specs=[tok(d), tok(A_GROUP_WIDTH), tok(B_WIDTH)] + [_resident(w.shape) for w in weights],
        out_specs=tok(d),
        out_shape=jax.ShapeDtypeStruct((bsz, seq, d), _F32),
        compiler_params=pltpu.CompilerParams(
            dimension_semantics=("parallel", "parallel"), vmem_limit_bytes=VMEM_LIMIT),
        name="post_ffn",
    )(x, ya, yb, *weights)


def _encoder_layer(x, p, consts, alpha):
    outs = _project(x, p["w_qkv"], consts["rope"])
    ya = _mixer_a(outs[:9], consts["a_masks"])
    yb = _mixer_b(*outs[9:], p["na_bias"])
    return _post(x, ya, yb, p, alpha)


def kernel(x_prompt, x_sample, w_in, b_gate, w_branch_a, w_branch_b, w_out, rel_pos_bias,
           ln1_g, ln1_b, w_ff1, b_ff1, w_ff2, b_ff2, ln2_g, ln2_b):
    depth = w_in.shape[0]
    alpha = (2.0 * depth) ** 0.25
    seq = x_prompt.shape[1]
    assert x_sample.shape[1] == seq and seq % (GRID_W * NA_KROWS) == 0
    consts = {"rope": _rope_tables(seq), "a_masks": _a_masks()}
    scale = HEAD_DIM ** -0.5 * LOG2E
    col_scale = np.ones((QKV_COLS,), np.float32)
    col_scale[0:A_WIDTH] = scale
    col_scale[3 * A_WIDTH:3 * A_WIDTH + B_WIDTH] = scale
    row = lambda v: v.reshape(1, -1)
    y_prompt, y_sample = x_prompt, x_sample
    for layer in range(depth):
        p = {
            "w_qkv": (w_in[layer][:, :QKV_COLS] * col_scale).astype(_BF16),
            "w_gate": w_in[layer][:, QKV_COLS:].astype(_BF16),
            "b_gate": b_gate[layer],
            "w_a": w_branch_a[layer].astype(_BF16),
            "w_b": w_branch_b[layer].astype(_BF16),
            "w_o": w_out[layer].astype(_BF16),
            "na_bias": _na_bias_table(rel_pos_bias[layer], seq // GRID_W),
            "ln1_g": row(ln1_g[layer]), "ln1_b": row(ln1_b[layer]),
            "w_ff1": w_ff1[layer].astype(_BF16), "b_ff1": row(b_ff1[layer]),
            "w_ff2": w_ff2[layer].astype(_BF16), "b_ff2": row(b_ff2[layer]),
            "ln2_g": row(ln2_g[layer]), "ln2_b": row(ln2_b[layer]),
        }
        y_prompt = _encoder_layer(y_prompt, p, consts, alpha)
        y_sample = _encoder_layer(y_sample, p, consts, alpha)
    return (y_prompt, y_sample)
```
